```python
import math
import jax, jax.numpy as jnp
from jax import lax
import numpy as np


D_MODEL = 2048
BATCH = 16
SEQ = 2048
DEPTH = 4

N_META = 16
RWKV_HEAD_DIM = 64
RWKV_WIDTH = 3 * D_MODEL // 8
RWKV_HEADS = RWKV_WIDTH // RWKV_HEAD_DIM
RWKV_DECAY_RANK = 96
RWKV_AAA_RANK = 96
RWKV_GATE_RANK = 256
RWKV_COLS = 3 * RWKV_WIDTH + RWKV_DECAY_RANK + RWKV_AAA_RANK + RWKV_GATE_RANK
RWKV_GN_EPS = 64e-5
FOX_HEAD_DIM = 64
FOX_WIDTH = 3 * D_MODEL // 8
FOX_HEADS = FOX_WIDTH // FOX_HEAD_DIM
FOX_BLOCK = 128
FOX_COLS = 3 * FOX_WIDTH + FOX_HEADS
MLSTM_WIDTH = D_MODEL // 4
MLSTM_HEADS = 4
MLSTM_HEAD_DIM = MLSTM_WIDTH // MLSTM_HEADS
MLSTM_CONV = 4
MLSTM_CHUNK = 64
MLSTM_COLS = 4 * MLSTM_WIDTH + 2 * MLSTM_HEADS
N_BRANCH = 3
GATE_COLS = N_BRANCH * D_MODEL
N_IN = RWKV_COLS + FOX_COLS + MLSTM_COLS + GATE_COLS
D_FF = 256 * math.ceil(8 * D_MODEL / 3 / 256)
N_EXPERTS = 8
TOP_K = 2
D_FF_EXPERT = D_FF
N_DENSE = (DEPTH + 1) // 2
N_MOE = DEPTH // 2
LN_EPS = 1e-5
DEEPNORM_ALPHA = (2 * DEPTH) ** 0.25
DEEPNORM_BETA = (8 * DEPTH) ** -0.25

kernel_name = 'hybrid_rwkv7_fox_mlstm_moe_deepnorm'


def layer_norm(x, g, b):
    xf = x.astype(jnp.float32)
    mu = jnp.mean(xf, axis=-1, keepdims=True)
    var = jnp.mean(jnp.square(xf - mu), axis=-1, keepdims=True)
    return ((xf - mu) * lax.rsqrt(var + LN_EPS) * g + b).astype(x.dtype)


def causal_depthwise_conv(x, w):
    K, L = w.shape[0], x.shape[1]
    xp = jnp.pad(x, ((0, 0), (K - 1, 0), (0, 0)))
    y = xp[:, 0:L] * w[0]
    for j in range(1, K):
        y = y + xp[:, j:j + L] * w[j]
    return y


def rwkv7_time_mix(z, mu, w0, w2, a0, a2, g2, k_k, k_a, r_k, gn_g, gn_b):
    f32 = jnp.float32
    B, L, _ = z.shape
    W = RWKV_WIDTH
    z_prev = jnp.pad(z, ((0, 0), (1, 0), (0, 0)))[:, :-1]
    z = z + mu * (z_prev - z)
    r, k, v, wd, ad, gd = jnp.split(
        z, [W, 2 * W, 3 * W, 3 * W + RWKV_DECAY_RANK, 3 * W + RWKV_DECAY_RANK + RWKV_AAA_RANK], axis=-1)
    w_log = -jax.nn.softplus(-(w0 + jnp.tanh(wd) @ w2).astype(f32)) - 0.5
    decay = jnp.exp(-jnp.exp(w_log))
    a = jax.nn.sigmoid((a0 + ad @ a2).astype(f32))
    g = jax.nn.sigmoid(gd) @ g2

    def heads(t):
        return t.reshape(B, L, RWKV_HEADS, RWKV_HEAD_DIM)

    kk = heads((k * k_k).astype(f32))
    kk = kk / jnp.maximum(jnp.sqrt(jnp.sum(jnp.square(kk), axis=-1, keepdims=True)), 1e-12)
    k = heads(k.astype(f32) * (1.0 + (a - 1.0) * k_a))
    r, v, decay, a = heads(r.astype(f32)), heads(v.astype(f32)), heads(decay), heads(a)

    def step(S, inp):
        r_t, w_t, k_t, v_t, kk_t, a_t = inp
        S = (S * w_t[:, :, None, :]
             - jnp.einsum('bhvk,bhk->bhv', S, kk_t)[..., None] * (kk_t * a_t)[:, :, None, :]
             + v_t[..., None] * k_t[:, :, None, :])
        return S, jnp.einsum('bhvk,bhk->bhv', S, r_t)

    xs = tuple(jnp.moveaxis(t, 1, 0) for t in (r, decay, k, v, kk, a))
    S0 = jnp.zeros((B, RWKV_HEADS, RWKV_HEAD_DIM, RWKV_HEAD_DIM), f32)
    _, o = lax.scan(step, S0, xs)
    o = jnp.moveaxis(o, 0, 1)
    mean = jnp.mean(o, axis=-1, keepdims=True)
    var = jnp.mean(jnp.square(o - mean), axis=-1, keepdims=True)
    o = ((o - mean) * lax.rsqrt(var + RWKV_GN_EPS)).reshape(B, L, W) * gn_g + gn_b
    bonus = jnp.sum(r * k * r_k, axis=-1, keepdims=True) * v
    o = (o + bonus.reshape(B, L, W)) * g
    return o.astype(z.dtype)


def forgetting_attention(z, b_f):
    B, L, _ = z.shape
    W = FOX_WIDTH
    q, k, v, fg = jnp.split(z, [W, 2 * W, 3 * W], axis=-1)
    q = q.reshape(B, L, FOX_HEADS, FOX_HEAD_DIM)
    k = k.reshape(B, L, FOX_HEADS, FOX_HEAD_DIM)
    v = v.reshape(B, L, FOX_HEADS, FOX_HEAD_DIM)
    log_f = jax.nn.log_sigmoid((fg + b_f).astype(jnp.float32))
    cum = jnp.cumsum(log_f, axis=1).transpose(0, 2, 1)
    scale = FOX_HEAD_DIM ** -0.5
    starts = [0] + list(range(N_META, L, FOX_BLOCK))
    ends = starts[1:] + [L]
    outs = []
    for s, e in zip(starts, ends):
        logits = jnp.einsum('bqhd,bkhd->bhqk', q[:, s:e], k[:, :e],
                            preferred_element_type=jnp.float32) * scale
        logits = logits + cum[:, :, s:e, None] - cum[:, :, None, :e]
        causal = jnp.arange(s, e)[:, None] >= jnp.arange(e)[None, :]
        logits = jnp.where(causal, logits, -jnp.inf)
        p = jax.nn.softmax(logits, axis=-1).astype(v.dtype)
        outs.append(jnp.einsum('bhqk,bkhd->bqhd', p, v[:, :e]))
    return jnp.concatenate(outs, axis=1).reshape(B, L, W)


def mlstm_chunk(carry, inputs):
    C, n, m = carry
    q, k, v, li, lf = inputs
    Lc = q.shape[2]
    b = jnp.cumsum(lf, axis=-1)
    causal = jnp.tril(jnp.ones((Lc, Lc), dtype=bool))
    log_d = jnp.where(causal, b[..., :, None] - b[..., None, :] + li[..., None, :], -jnp.inf)
    log_inter = b + m[..., None]
    m_q = jnp.maximum(log_inter, jnp.max(log_d, axis=-1))
    s = jnp.einsum('bhqd,bhkd->bhqk', q, k) * jnp.exp(log_d - m_q[..., None])
    w_inter = jnp.exp(log_inter - m_q)
    num = w_inter[..., None] * jnp.einsum('bhqd,bhdv->bhqv', q, C) + jnp.einsum('bhqk,bhkv->bhqv', s, v)
    den = w_inter * jnp.einsum('bhqd,bhd->bhq', q, n) + jnp.sum(s, axis=-1)
    h = num / jnp.maximum(jnp.abs(den), jnp.exp(-m_q))[..., None]
    b_end = b[..., -1]
    log_g = b_end[..., None] - b + li
    m_new = jnp.maximum(b_end + m, jnp.max(log_g, axis=-1))
    gk = jnp.exp(log_g - m_new[..., None])
    dec = jnp.exp(b_end + m - m_new)
    C = dec[..., None, None] * C + jnp.einsum('bhl,bhld,bhlv->bhdv', gk, k, v)
    n = dec[..., None] * n + jnp.einsum('bhl,bhld->bhd', gk, k)
    return (C, n, m_new), h


def mlstm_mix(z, conv_w, b_i, b_f):
    f32 = jnp.float32
    B, L, _ = z.shape
    W, H, Dh = MLSTM_WIDTH, MLSTM_HEADS, MLSTM_HEAD_DIM
    qk, v, o, ig, fg = jnp.split(z, [2 * W, 3 * W, 4 * W, 4 * W + H], axis=-1)
    qk = jax.nn.silu(causal_depthwise_conv(qk, conv_w))
    q, k = jnp.split(qk, 2, axis=-1)

    def heads(t):
        return t.astype(f32).reshape(B, L, H, Dh).transpose(0, 2, 1, 3)

    q, k, v = heads(q), heads(k) * (Dh ** -0.5), heads(v)
    li = (ig + b_i).astype(f32).transpose(0, 2, 1)
    lf = jax.nn.log_sigmoid((fg + b_f).astype(f32)).transpose(0, 2, 1)
    carry = (jnp.zeros((B, H, Dh, Dh), f32), jnp.zeros((B, H, Dh), f32), jnp.zeros((B, H), f32))
    carry, h_meta = mlstm_chunk(carry, (q[:, :, :N_META], k[:, :, :N_META], v[:, :, :N_META],
                                        li[:, :, :N_META], lf[:, :, :N_META]))
    n_chunks = (L - N_META) // MLSTM_CHUNK

    def to_chunks(t):
        t = t[:, :, N_META:]
        t = t.reshape((B, H, n_chunks, MLSTM_CHUNK) + t.shape[3:])
        return jnp.moveaxis(t, 2, 0)

    _, h_rest = lax.scan(mlstm_chunk, carry, tuple(to_chunks(t) for t in (q, k, v, li, lf)))
    h_rest = jnp.moveaxis(h_rest, 0, 2).reshape(B, H, L - N_META, Dh)
    h = jnp.concatenate([h_meta, h_rest], axis=2).transpose(0, 2, 1, 3).reshape(B, L, W)
    return (jax.nn.sigmoid(o.astype(f32)) * h).astype(z.dtype)


def swiglu(x, w1, w3, w2):
    return (jax.nn.silu(x @ w1) * (x @ w3)) @ w2


def moe_swiglu(x, router_w, w1, w3, w2):
    logits = (x @ router_w).astype(jnp.float32)
    top_v, top_i = lax.top_k(logits, TOP_K)
    top_w = jax.nn.softmax(top_v, axis=-1)
    gates = jnp.sum(jax.nn.one_hot(top_i, N_EXPERTS, dtype=jnp.float32) * top_w[..., None],
                    axis=-2).astype(x.dtype)
    y = jnp.zeros_like(x)
    for e in range(N_EXPERTS):
        y = y + gates[..., e:e + 1] * swiglu(x, w1[e], w3[e], w2[e])
    return y


def setup_inputs(seed: int = 0) -> dict:
    key = jax.random.key(seed)
    keys = iter(jax.random.split(key, 64))

    def nrm(shape, scale=1.0):
        return jax.random.normal(next(keys), shape, jnp.float32) * scale

    def unif(shape):
        return jax.random.uniform(next(keys), shape, jnp.float32)

    D, L = D_MODEL, DEPTH
    return {
        'x': nrm((BATCH, SEQ, D)),
        'meta_tokens': nrm((N_META, D)),
        'ln_emb_g': 1.0 + nrm((D,), 0.05),
        'ln_emb_b': nrm((D,), 0.02),
        'w_in': nrm((L, D, N_IN), D ** -0.5),
        'rwkv_mu': unif((L, RWKV_COLS)),
        'rwkv_w0': nrm((L, RWKV_WIDTH), 0.5),
        'rwkv_w2': nrm((L, RWKV_DECAY_RANK, RWKV_WIDTH), 0.1 * RWKV_DECAY_RANK ** -0.5),
        'rwkv_a0': nrm((L, RWKV_WIDTH), 0.1),
        'rwkv_a2': nrm((L, RWKV_AAA_RANK, RWKV_WIDTH), 0.1 * RWKV_AAA_RANK ** -0.5),
        'rwkv_g2': nrm((L, RWKV_GATE_RANK, RWKV_WIDTH), RWKV_GATE_RANK ** -0.5),
        'rwkv_k_k': 0.85 + nrm((L, RWKV_WIDTH), 0.05),
        'rwkv_k_a': 1.0 + nrm((L, RWKV_WIDTH), 0.05),
        'rwkv_r_k': nrm((L, RWKV_HEADS, RWKV_HEAD_DIM), 0.1),
        'rwkv_gn_g': 1.0 + nrm((L, RWKV_WIDTH), 0.05),
        'rwkv_gn_b': nrm((L, RWKV_WIDTH), 0.02),
        'fox_b_f': jnp.linspace(1.0, 4.0, FOX_HEADS, dtype=jnp.float32) + nrm((L, FOX_HEADS), 0.1),
        'mlstm_conv_w': nrm((L, MLSTM_CONV, 2 * MLSTM_WIDTH), MLSTM_CONV ** -0.5),
        'mlstm_b_i': nrm((L, MLSTM_HEADS), 0.1),
        'mlstm_b_f': jnp.linspace(3.0, 6.0, MLSTM_HEADS, dtype=jnp.float32) + nrm((L, MLSTM_HEADS), 0.1),
        'proj_rwkv': nrm((L, RWKV_WIDTH, D), DEEPNORM_BETA * RWKV_WIDTH ** -0.5),
        'proj_fox': nrm((L, FOX_WIDTH, D), DEEPNORM_BETA * FOX_WIDTH ** -0.5),
        'proj_mlstm': nrm((L, MLSTM_WIDTH, D), DEEPNORM_BETA * MLSTM_WIDTH ** -0.5),
        'w_out': nrm((L, D, D), DEEPNORM_BETA * D ** -0.5),
        'ln1_g': 1.0 + nrm((L, D), 0.05),
        'ln1_b': nrm((L, D), 0.02),
        'ffn_w1': nrm((N_DENSE, D, D_FF), D ** -0.5),
        'ffn_w3': nrm((N_DENSE, D, D_FF), D ** -0.5),
        'ffn_w2': nrm((N_DENSE, D_FF, D), DEEPNORM_BETA * D_FF ** -0.5),
        'router_w': nrm((N_MOE, D, N_EXPERTS), D ** -0.5),
        'moe_w1': nrm((N_MOE, N_EXPERTS, D, D_FF_EXPERT), D ** -0.5),
        'moe_w3': nrm((N_MOE, N_EXPERTS, D, D_FF_EXPERT), D ** -0.5),
        'moe_w2': nrm((N_MOE, N_EXPERTS, D_FF_EXPERT, D), DEEPNORM_BETA * D_FF_EXPERT ** -0.5),
        'ln2_g': 1.0 + nrm((L, D), 0.05),
        'ln2_b': nrm((L, D), 0.02),
    }


def reference(x, meta_tokens, ln_emb_g, ln_emb_b, w_in, rwkv_mu, rwkv_w0, rwkv_w2, rwkv_a0, rwkv_a2,
              rwkv_g2, rwkv_k_k, rwkv_k_a, rwkv_r_k, rwkv_gn_g, rwkv_gn_b, fox_b_f, mlstm_conv_w,
              mlstm_b_i, mlstm_b_f, proj_rwkv, proj_fox, proj_mlstm, w_out, ln1_g, ln1_b,
              ffn_w1, ffn_w3, ffn_w2, router_w, moe_w1, moe_w3, moe_w2, ln2_g, ln2_b):
    B = x.shape[0]
    meta = jnp.broadcast_to(meta_tokens[None].astype(x.dtype), (B, N_META, D_MODEL))
    h = layer_norm(jnp.concatenate([meta, x], axis=1), ln_emb_g, ln_emb_b)
    c1 = RWKV_COLS
    c2 = c1 + FOX_COLS
    c3 = c2 + MLSTM_COLS
    for l in range(DEPTH):
        z = h @ w_in[l]
        z_r, z_f, z_m, z_g = jnp.split(z, [c1, c2, c3], axis=-1)
        y_r = rwkv7_time_mix(z_r, rwkv_mu[l], rwkv_w0[l], rwkv_w2[l], rwkv_a0[l], rwkv_a2[l],
                             rwkv_g2[l], rwkv_k_k[l], rwkv_k_a[l], rwkv_r_k[l], rwkv_gn_g[l], rwkv_gn_b[l])
        y_f = forgetting_attention(z_f, fox_b_f[l])
        y_m = mlstm_mix(z_m, mlstm_conv_w[l], mlstm_b_i[l], mlstm_b_f[l])
        g_r, g_f, g_m = jnp.split(jax.nn.sigmoid(z_g), N_BRANCH, axis=-1)
        mixed = (g_r * (y_r @ proj_rwkv[l]) + g_f * (y_f @ proj_fox[l])
                 + g_m * (y_m @ proj_mlstm[l])) @ w_out[l]
        h = layer_norm(DEEPNORM_ALPHA * h + mixed, ln1_g[l], ln1_b[l])
        if l % 2 == 0:
            ff = swiglu(h, ffn_w1[l // 2], ffn_w3[l // 2], ffn_w2[l // 2])
        else:
            ff = moe_swiglu(h, router_w[l // 2], moe_w1[l // 2], moe_w3[l // 2], moe_w2[l // 2])
        h = layer_norm(DEEPNORM_ALPHA * h + ff, ln2_g[l], ln2_b[l])
    return h[:, N_META:]
```

```python
import functools
import math

import jax
import jax.numpy as jnp
from jax import lax
from jax.experimental import pallas as pl
from jax.experimental.pallas import tpu as pltpu

F32 = jnp.float32
BF16 = jnp.bfloat16
HI = lax.Precision.HIGHEST

N_META = 16
DEPTH = 4
RWKV_HEAD_DIM = 64
RWKV_DECAY_RANK = 96
RWKV_AAA_RANK = 96
RWKV_GATE_RANK = 256
RWKV_GN_EPS = 64e-5
FOX_HEAD_DIM = 64
MLSTM_HEADS = 4
MLSTM_CONV = 4
N_EXPERTS = 8
LN_EPS = 1e-5
DEEPNORM_ALPHA = (2 * DEPTH) ** 0.25

CHUNK = 64
LANES = 128
SUBLANES = 8
ATT_BLOCK = 256
VMEM_LIMIT = 56 * 1024 * 1024
NEG_BIG = -1e30


def _cparams(sem):
    return pltpu.CompilerParams(dimension_semantics=sem, vmem_limit_bytes=VMEM_LIMIT)


def _tile(n, pref, mult=SUBLANES):
    best = None
    for d in range(mult, min(n, pref) + 1, mult):
        if n % d == 0:
            best = d
    return best if best is not None else n


def _dot(a, b, precision=None):
    return jnp.dot(a, b, preferred_element_type=F32, precision=precision)


def _dot_nt(a, b, precision=None):
    return lax.dot_general(a, b, (((1,), (1,)), ((), ())), preferred_element_type=F32,
                           precision=precision)


def _dot_tn(a, b, precision=None):
    return lax.dot_general(a, b, (((0,), (0,)), ((), ())), preferred_element_type=F32,
                           precision=precision)


def _softplus(x):
    return jnp.maximum(x, 0.0) + jnp.log(1.0 + jnp.exp(-jnp.abs(x)))


def _log_sigmoid(x):
    return jnp.minimum(x, 0.0) - jnp.log(1.0 + jnp.exp(-jnp.abs(x)))


def _sigmoid(x):
    return 1.0 / (1.0 + jnp.exp(-x))


def _iota(shape, dim):
    return lax.broadcasted_iota(jnp.int32, shape, dim)


def _layer_norm(t, g, b):
    mu = jnp.mean(t, axis=-1, keepdims=True)
    d = t - mu
    var = jnp.mean(d * d, axis=-1, keepdims=True)
    return d * lax.rsqrt(var + LN_EPS) * g + b


def _ln_kernel(x_ref, g_ref, b_ref, o_ref, ob_ref):
    y = _layer_norm(x_ref[...], g_ref[...], b_ref[...])
    o_ref[...] = y
    ob_ref[...] = y.astype(BF16)


def _ln(x, g, b):
    T, D = x.shape
    bm = _tile(T, 512)
    return pl.pallas_call(
        _ln_kernel,
        grid=(T // bm,),
        in_specs=[pl.BlockSpec((bm, D), lambda i: (i, 0)),
                  pl.BlockSpec((1, D), lambda i: (0, 0)),
                  pl.BlockSpec((1, D), lambda i: (0, 0))],
        out_specs=[pl.BlockSpec((bm, D), lambda i: (i, 0)),
                   pl.BlockSpec((bm, D), lambda i: (i, 0))],
        out_shape=[jax.ShapeDtypeStruct((T, D), F32), jax.ShapeDtypeStruct((T, D), BF16)],
        compiler_params=_cparams(("parallel",)),
        name="ln_embed",
    )(x, g.reshape(1, D), b.reshape(1, D))


def _mm_kernel(a_ref, b_ref, o_ref):
    o_ref[...] = _dot(a_ref[...], b_ref[...]).astype(o_ref.dtype)


def _mm(a, b, out_dtype, bn_pref, name):
    M, K = a.shape
    N = b.shape[1]
    bm = _tile(M, 1024)
    bn = _tile(N, bn_pref, LANES)
    return pl.pallas_call(
        _mm_kernel,
        grid=(M // bm, N // bn),
        in_specs=[pl.BlockSpec((bm, K), lambda i, j: (i, 0)),
                  pl.BlockSpec((K, bn), lambda i, j: (0, j))],
        out_specs=pl.BlockSpec((bm, bn), lambda i, j: (i, j)),
        out_shape=jax.ShapeDtypeStruct((M, N), out_dtype),
        compiler_params=_cparams(("parallel", "arbitrary")),
        name=name,
    )(a, b)


def _mm_res_ln_kernel(a_ref, w_ref, h_ref, g_ref, b_ref, o_ref, ob_ref):
    t = DEEPNORM_ALPHA * h_ref[...] + _dot(a_ref[...], w_ref[...])
    y = _layer_norm(t, g_ref[...], b_ref[...])
    o_ref[...] = y
    ob_ref[...] = y.astype(BF16)


def _mm_res_ln(a, w, h, g, b):
    M, K = a.shape
    D = w.shape[1]
    bm = _tile(M, 512)
    return pl.pallas_call(
        _mm_res_ln_kernel,
        grid=(M // bm,),
        in_specs=[pl.BlockSpec((bm, K), lambda i: (i, 0)),
                  pl.BlockSpec((K, D), lambda i: (0, 0)),
                  pl.BlockSpec((bm, D), lambda i: (i, 0)),
                  pl.BlockSpec((1, D), lambda i: (0, 0)),
                  pl.BlockSpec((1, D), lambda i: (0, 0))],
        out_specs=[pl.BlockSpec((bm, D), lambda i: (i, 0)),
                   pl.BlockSpec((bm, D), lambda i: (i, 0))],
        out_shape=[jax.ShapeDtypeStruct((M, D), F32), jax.ShapeDtypeStruct((M, D), BF16)],
        compiler_params=_cparams(("parallel",)),
        name="out_proj_ln",
    )(a, w, h, g.reshape(1, D), b.reshape(1, D))


def _merge_kernel(hb_ref, wg_ref, yr_ref, yf_ref, ym_ref, pr_ref, pf_ref, pm_ref, o_ref):
    hb = hb_ref[...]
    acc = _sigmoid(_dot(hb, wg_ref[0])) * _dot(yr_ref[...], pr_ref[...])
    acc += _sigmoid(_dot(hb, wg_ref[1])) * _dot(yf_ref[...], pf_ref[...])
    acc += _sigmoid(_dot(hb, wg_ref[2])) * _dot(ym_ref[...], pm_ref[...])
    o_ref[...] = acc.astype(o_ref.dtype)


def _merge(hb, wg, yr, yf, ym, pr, pf, pm):
    M, D = hb.shape
    bm = _tile(M, 1024)
    bn = _tile(D, 512, LANES)
    row = lambda w: pl.BlockSpec((bm, w), lambda i, j: (i, 0))
    col = lambda k: pl.BlockSpec((k, bn), lambda i, j: (0, j))
    return pl.pallas_call(
        _merge_kernel,
        grid=(M // bm, D // bn),
        in_specs=[row(D), pl.BlockSpec((3, D, bn), lambda i, j: (0, 0, j)),
                  row(yr.shape[1]), row(yf.shape[1]), row(ym.shape[1]),
                  col(pr.shape[0]), col(pf.shape[0]), col(pm.shape[0])],
        out_specs=pl.BlockSpec((bm, bn), lambda i, j: (i, j)),
        out_shape=jax.ShapeDtypeStruct((M, D), BF16),
        compiler_params=_cparams(("parallel", "arbitrary")),
        name="merge",
    )(hb, wg, yr, yf, ym, pr, pf, pm)


def _router_kernel(h_ref, w_ref, o_ref):
    logits = _dot(h_ref[...], w_ref[...], HI)
    lane = _iota(logits.shape, 1)
    logits = jnp.where(lane < N_EXPERTS, logits, -jnp.inf)
    m1 = jnp.max(logits, axis=-1, keepdims=True)
    i1 = jnp.min(jnp.where(logits == m1, lane, LANES), axis=-1, keepdims=True)
    rest = jnp.where(lane == i1, -jnp.inf, logits)
    m2 = jnp.max(rest, axis=-1, keepdims=True)
    i2 = jnp.min(jnp.where(rest == m2, lane, LANES), axis=-1, keepdims=True)
    e2 = jnp.exp(m2 - m1)
    den = 1.0 + e2
    o_ref[...] = jnp.where(lane == i1, 1.0 / den, 0.0) + jnp.where(lane == i2, e2 / den, 0.0)


def _router(h, w):
    T, D = h.shape
    bm = _tile(T, 512)
    wp = jnp.zeros((D, LANES), F32).at[:, :N_EXPERTS].set(w)
    return pl.pallas_call(
        _router_kernel,
        grid=(T // bm,),
        in_specs=[pl.BlockSpec((bm, D), lambda i: (i, 0)),
                  pl.BlockSpec((D, LANES), lambda i: (0, 0))],
        out_specs=pl.BlockSpec((bm, LANES), lambda i: (i, 0)),
        out_shape=jax.ShapeDtypeStruct((T, LANES), F32),
        compiler_params=_cparams(("parallel",)),
        name="router",
    )(h, wp)


def _ffn_kernel(*refs, gated):
    if gated:
        x_ref, w1_ref, w3_ref, w2_ref, gt_ref, h_ref, g_ref, b_ref, o_ref, ob_ref, acc_ref = refs
    else:
        x_ref, w1_ref, w3_ref, w2_ref, h_ref, g_ref, b_ref, o_ref, ob_ref, acc_ref = refs
    e = pl.program_id(1)
    j = pl.program_id(2)

    @pl.when((e == 0) & (j == 0))
    def _():
        acc_ref[...] = jnp.zeros_like(acc_ref)

    x = x_ref[...]
    a = _dot(x, w1_ref[0])
    c = _dot(x, w3_ref[0])
    hh = a * _sigmoid(a) * c
    if gated:
        gates = gt_ref[...]
        lane = _iota(gates.shape, 1)
        hh = hh * jnp.sum(jnp.where(lane == e, gates, 0.0), axis=-1, keepdims=True)
    acc_ref[...] += _dot(hh.astype(BF16), w2_ref[0])

    @pl.when((e == pl.num_programs(1) - 1) & (j == pl.num_programs(2) - 1))
    def _():
        y = _layer_norm(DEEPNORM_ALPHA * h_ref[...] + acc_ref[...], g_ref[...], b_ref[...])
        o_ref[...] = y
        ob_ref[...] = y.astype(BF16)


def _ffn(xb, w1, w3, w2, gates, h, g, b):
    M, D = xb.shape
    E, _, FF = w1.shape
    bm = _tile(M, 512)
    bf = _tile(FF, 512, LANES)
    gated = gates is not None
    row = lambda w: pl.BlockSpec((bm, w), lambda i, e, j: (i, 0))
    vec = pl.BlockSpec((1, D), lambda i, e, j: (0, 0))
    in_specs = [row(D),
                pl.BlockSpec((1, D, bf), lambda i, e, j: (e, 0, j)),
                pl.BlockSpec((1, D, bf), lambda i, e, j: (e, 0, j)),
                pl.BlockSpec((1, bf, D), lambda i, e, j: (e, j, 0))]
    args = [xb, w1, w3, w2]
    if gated:
        in_specs.append(row(LANES))
        args.append(gates)
    in_specs += [row(D), vec, vec]
    args += [h, g.reshape(1, D), b.reshape(1, D)]
    return pl.pallas_call(
        functools.partial(_ffn_kernel, gated=gated),
        grid=(M // bm, E, FF // bf),
        in_specs=in_specs,
        out_specs=[row(D), row(D)],
        out_shape=[jax.ShapeDtypeStruct((M, D), F32), jax.ShapeDtypeStruct((M, D), BF16)],
        scratch_shapes=[pltpu.VMEM((bm, D), F32)],
        compiler_params=_cparams(("parallel", "arbitrary", "arbitrary")),
        name="moe_ffn" if gated else "ffn",
    )(*args)


RWKV_GROUP = 4
RWKV_GW = RWKV_GROUP * RWKV_HEAD_DIM
RWKV_LORA_W = 512


def _shift_lerp(ref, r0, p0, notfirst, mu, row0):
    cur = ref[0, pl.ds(r0, CHUNK), :]
    prev = ref[0, pl.ds(p0, SUBLANES), :][SUBLANES - 1:SUBLANES, :] * notfirst
    zprev = jnp.where(row0, prev, pltpu.roll(cur, 1, 0))
    return cur + mu * (zprev - cur)


def _rwkv_kernel(lo_ref, r_ref, k_ref, v_ref, vec_ref, mul_ref, w2_ref, a2_ref, g2_ref,
                 o_ref, s_ref, oc_ref):
    C, N, GW = CHUNK, RWKV_HEAD_DIM, RWKV_GW
    nchunk = r_ref.shape[1] // C
    s_ref[...] = jnp.zeros_like(s_ref)

    ri = _iota((C, C), 0)
    ci = _iota((C, C), 1)
    strict = ri > ci
    incl = ri >= ci
    tri = incl.astype(F32)
    eye = (ri == ci).astype(F32)
    bd = (_iota((GW, GW), 0) // N == _iota((GW, GW), 1) // N).astype(F32)
    row0_w = _iota((C, GW), 0) == 0
    row0_l = _iota((C, RWKV_LORA_W), 0) == 0

    w0 = vec_ref[0:1, :]
    a0 = vec_ref[1:2, :]
    k_k = vec_ref[2:3, :]
    k_a = vec_ref[3:4, :]
    r_k = vec_ref[4:5, :]
    gn_g = vec_ref[5:6, :]
    gn_b = vec_ref[6:7, :]
    mu_r = vec_ref[8:9, :]
    mu_k = vec_ref[9:10, :]
    mu_v = vec_ref[10:11, :]
    mu_l = mul_ref[0:1, :]

    def body(c, carry):
        r0 = pl.multiple_of(c * C, C)
        p0 = pl.multiple_of(jnp.maximum(r0 - SUBLANES, 0), SUBLANES)
        notfirst = (c > 0).astype(F32)
        lo = _shift_lerp(lo_ref, r0, p0, notfirst, mu_l, row0_l)
        r = _shift_lerp(r_ref, r0, p0, notfirst, mu_r, row0_w)
        k = _shift_lerp(k_ref, r0, p0, notfirst, mu_k, row0_w)
        v = _shift_lerp(v_ref, r0, p0, notfirst, mu_v, row0_w)

        wd = lo[:, 0:128]
        ad = lo[:, 128:256]
        gd = lo[:, 256:512]
        w_log = -_softplus(-(w0 + _dot(jnp.tanh(wd).astype(BF16), w2_ref[...]))) - 0.5
        lw = -jnp.exp(w_log)
        a = _sigmoid(a0 + _dot(ad.astype(BF16), a2_ref[...]))
        g = _dot(_sigmoid(gd).astype(BF16), g2_ref[...])

        kk = k * k_k
        kk = kk / jnp.maximum(jnp.sqrt(_dot(kk * kk, bd, HI)), 1e-12)
        kmod = k * (1.0 + (a - 1.0) * k_a)
        b = kk * a

        cum = _dot(tri, lw, HI)
        cum_c = cum[C - 1:C, :]
        e_inv = jnp.exp(-cum)
        e_fin = jnp.exp(cum_c - cum)
        kkt = kk * jnp.exp(cum - lw)
        rt = r * jnp.exp(cum)
        bt = b * e_inv
        kt = kmod * e_inv
        bh = b * e_fin
        kh = kmod * e_fin
        pc = jnp.exp(cum_c)

        for h in range(RWKV_GROUP):
            sl = slice(h * N, (h + 1) * N)
            lhs = jnp.concatenate([kkt[:, sl], rt[:, sl]], axis=0).astype(BF16)
            rhs = jnp.concatenate([bt[:, sl], kt[:, sl]], axis=0).astype(BF16)
            mall = _dot_nt(lhs, rhs)
            ab = jnp.where(strict, mall[:C, :C], 0.0)
            ak = jnp.where(strict, mall[:C, C:], 0.0)
            mb = jnp.where(incl, mall[C:, :C], 0.0)
            mk = jnp.where(incl, mall[C:, C:], 0.0)
            x = -ab
            t = eye + x
            for _ in range(5):
                xb = x.astype(BF16)
                x = _dot(xb, xb)
                t = t + _dot(t.astype(BF16), x.astype(BF16))
            s_old = s_ref[h]
            gm = _dot_nt(lhs, s_old.astype(BF16))
            vh = v[:, sl]
            av = _dot(jnp.concatenate([ak, mk], axis=0).astype(BF16), vh.astype(BF16))
            u = _dot(t.astype(BF16), (gm[:C] + av[:C]).astype(BF16))
            oc_ref[:, sl] = gm[C:] + av[C:] - _dot(mb.astype(BF16), u.astype(BF16))
            upd = _dot_tn(jnp.concatenate([u, vh], axis=0).astype(BF16),
                          jnp.concatenate([-bh[:, sl], kh[:, sl]], axis=0).astype(BF16))
            s_ref[h] = s_old * pc[:, sl] + upd

        o = oc_ref[...]
        mean = _dot(o, bd, HI) * (1.0 / N)
        d = o - mean
        var = _dot(d * d, bd, HI) * (1.0 / N)
        on = d * lax.rsqrt(var + RWKV_GN_EPS) * gn_g + gn_b
        bonus = _dot(r * kmod * r_k, bd, HI) * v
        o_ref[0, pl.ds(r0, C), :] = ((on + bonus) * g).astype(o_ref.dtype)
        return carry

    lax.fori_loop(0, nchunk, body, 0)


def _rwkv(z, vecs, mul, w2, a2, g2, B, LP):
    W = vecs.shape[1]
    ng = W // RWKV_GW
    nl = RWKV_LORA_W // RWKV_GW
    seq = lambda off: pl.BlockSpec((1, LP, RWKV_GW), lambda b, g: (b, 0, off + g))
    return pl.pallas_call(
        _rwkv_kernel,
        grid=(B, ng),
        in_specs=[pl.BlockSpec((1, LP, RWKV_LORA_W), lambda b, g: (b, 0, 0)),
                  seq(nl), seq(nl + ng), seq(nl + 2 * ng),
                  pl.BlockSpec((16, RWKV_GW), lambda b, g: (0, g)),
                  pl.BlockSpec((8, RWKV_LORA_W), lambda b, g: (0, 0)),
                  pl.BlockSpec((LANES, RWKV_GW), lambda b, g: (0, g)),
                  pl.BlockSpec((LANES, RWKV_GW), lambda b, g: (0, g)),
                  pl.BlockSpec((RWKV_GATE_RANK, RWKV_GW), lambda b, g: (0, g))],
        out_specs=pl.BlockSpec((1, LP, RWKV_GW), lambda b, g: (b, 0, g)),
        out_shape=jax.ShapeDtypeStruct((B, LP, W), BF16),
        scratch_shapes=[pltpu.VMEM((RWKV_GROUP, RWKV_HEAD_DIM, RWKV_HEAD_DIM), F32),
                        pltpu.VMEM((CHUNK, RWKV_GW), F32)],
        compiler_params=_cparams(("parallel", "parallel")),
        name="rwkv7",
    )(z, z, z, z, vecs, mul, w2, a2, g2)


def _fox_cum_kernel(zs_ref, bias_ref, o_ref):
    LP = zs_ref.shape[1]
    nkb = o_ref.shape[2]
    TB = ATT_BLOCK
    o_ref[...] = jnp.zeros_like(o_ref)
    sel = (_iota((16, LANES), 0) == _iota((16, LANES), 1)).astype(F32)
    carry = jnp.zeros((16, 1), F32)
    for kb in range(nkb):
        n = min(TB, LP - kb * TB)
        lf = _log_sigmoid(zs_ref[0, kb * TB:kb * TB + n, :] + bias_ref[...])
        lft = _dot_nt(sel, lf, HI)
        upper = (_iota((n, n), 0) <= _iota((n, n), 1)).astype(F32)
        cs = _dot(lft, upper, HI) + carry
        carry = cs[:, n - 1:n]
        for hp in range(o_ref.shape[1]):
            o_ref[0, hp, kb, 0:2, 0:n] = cs[2 * hp:2 * hp + 2, :]


def _fox_cum(zs, bias, B, LP, n_pairs):
    nkb = pl.cdiv(LP, ATT_BLOCK)
    return pl.pallas_call(
        _fox_cum_kernel,
        grid=(B,),
        in_specs=[pl.BlockSpec((1, LP, LANES), lambda b: (b, 0, 0)),
                  pl.BlockSpec((1, LANES), lambda b: (0, 0))],
        out_specs=pl.BlockSpec((1, n_pairs, nkb, SUBLANES, ATT_BLOCK), lambda b: (b, 0, 0, 0, 0)),
        out_shape=jax.ShapeDtypeStruct((B, n_pairs, nkb, SUBLANES, ATT_BLOCK), F32),
        compiler_params=_cparams(("parallel",)),
        name="fox_cum",
    )(zs, bias)


def _fox_kernel(q_ref, k_ref, v_ref, ct_ref, o_ref):
    TB, N = ATT_BLOCK, FOX_HEAD_DIM
    LP = k_ref.shape[1]
    n_full = LP // TB
    tail = LP - n_full * TB
    qi = pl.program_id(2)
    q = q_ref[0] * (N ** -0.5)
    qh = [q[:, h * N:(h + 1) * N].astype(BF16) for h in range(2)]

    def step(h, kblk, vblk, ck, mask, state):
        m, l, acc = state
        s = _dot_nt(qh[h], kblk[:, h * N:(h + 1) * N].astype(BF16)) - ck
        if mask is not None:
            s = jnp.where(mask, s, -jnp.inf)
        m_new = jnp.maximum(m, jnp.max(s, axis=-1, keepdims=True))
        p = jnp.exp(s - m_new)
        alpha = jnp.exp(m - m_new)
        l = alpha * l + jnp.sum(p, axis=-1, keepdims=True)
        acc = alpha * acc + _dot(p.astype(BF16), vblk[:, h * N:(h + 1) * N].astype(BF16))
        return m_new, l, acc

    def full_body(kb, states):
        k0 = pl.multiple_of(kb * TB, TB)
        kblk = k_ref[0, pl.ds(k0, TB), :]
        vblk = v_ref[0, pl.ds(k0, TB), :]
        return tuple(step(h, kblk, vblk, ct_ref[0, 0, kb, h:h + 1, :], None, states[h])
                     for h in range(2))

    init = tuple((jnp.full((TB, 1), NEG_BIG, F32), jnp.zeros((TB, 1), F32),
                  jnp.zeros((TB, N), F32)) for _ in range(2))
    states = lax.fori_loop(0, qi, full_body, init)

    def finish(n):
        k0 = pl.multiple_of(qi * TB, TB)
        kblk = k_ref[0, pl.ds(k0, n), :]
        vblk = v_ref[0, pl.ds(k0, n), :]
        mask = _iota((TB, n), 0) >= _iota((TB, n), 1)
        outs = []
        for h in range(2):
            ck = ct_ref[0, 0, qi, h:h + 1, :][:, 0:n]
            _, l, acc = step(h, kblk, vblk, ck, mask, states[h])
            outs.append(acc / l)
        o_ref[0] = jnp.concatenate(outs, axis=1).astype(o_ref.dtype)

    if tail == 0:
        finish(TB)
    else:
        pl.when(qi < n_full)(lambda: finish(TB))
        pl.when(qi == n_full)(lambda: finish(tail))


def _fox(zq, ct, B, LP, width):
    n_pairs = width // LANES
    nq = pl.cdiv(LP, ATT_BLOCK)
    nkb = ct.shape[2]
    kv = lambda off: pl.BlockSpec((1, LP, LANES), lambda b, p, i: (b, 0, off + p))
    return pl.pallas_call(
        _fox_kernel,
        grid=(B, n_pairs, nq),
        in_specs=[pl.BlockSpec((1, ATT_BLOCK, LANES), lambda b, p, i: (b, i, p)),
                  kv(n_pairs), kv(2 * n_pairs),
                  pl.BlockSpec((1, 1, nkb, SUBLANES, ATT_BLOCK), lambda b, p, i: (b, p, 0, 0, 0))],
        out_specs=pl.BlockSpec((1, ATT_BLOCK, LANES), lambda b, p, i: (b, i, p)),
        out_shape=jax.ShapeDtypeStruct((B, LP, width), BF16),
        compiler_params=_cparams(("parallel", "parallel", "arbitrary")),
        name="fox_attention",
    )(zq, zq, zq, ct)


MLSTM_GATE_LANE_I = 12
MLSTM_GATE_LANE_F = 16


def _conv_silu(ref, r0, p0, notfirst, w_ref):
    cur = ref[0, pl.ds(r0, CHUNK), :]
    prev = ref[0, pl.ds(p0, SUBLANES), :] * notfirst
    row8 = _iota(prev.shape, 0)
    y = cur * w_ref[MLSTM_CONV - 1:MLSTM_CONV, :]
    for s in range(1, MLSTM_CONV):
        rolled = pltpu.roll(cur, s, 0)
        top = jnp.where(row8 < s, pltpu.roll(prev, s, 0), rolled[0:SUBLANES])
        shifted = jnp.concatenate([top, rolled[SUBLANES:]], axis=0)
        y = y + shifted * w_ref[MLSTM_CONV - 1 - s:MLSTM_CONV - s, :]
    return y * _sigmoid(y)


def _mlstm_kernel(q_ref, k_ref, v_ref, og_ref, zs_ref, bias_ref, cwq_ref, cwk_ref,
                  o_ref, c_ref, m_ref):
    C, H = CHUNK, MLSTM_HEADS
    Dh = q_ref.shape[2] // H
    nchunk = q_ref.shape[1] // C
    c_ref[...] = jnp.zeros_like(c_ref)
    m_ref[...] = jnp.zeros_like(m_ref)

    ri = _iota((C, C), 0)
    ci = _iota((C, C), 1)
    incl = ri >= ci
    tri = incl.astype(F32)
    upper = (ri <= ci).astype(F32)
    sel = (_iota((SUBLANES, LANES), 1) == _iota((SUBLANES, LANES), 0) + MLSTM_GATE_LANE_I).astype(F32)
    ones_col = (_iota((C, LANES), 1) == 0).astype(BF16)
    lane = _iota((C, LANES), 1)

    def body(c, carry):
        r0 = pl.multiple_of(c * C, C)
        p0 = pl.multiple_of(jnp.maximum(r0 - SUBLANES, 0), SUBLANES)
        notfirst = (c > 0).astype(F32)
        q = _conv_silu(q_ref, r0, p0, notfirst, cwq_ref)
        k = _conv_silu(k_ref, r0, p0, notfirst, cwk_ref) * (Dh ** -0.5)
        v = v_ref[0, pl.ds(r0, C), :]
        og = og_ref[0, pl.ds(r0, C), :]
        gz = zs_ref[0, pl.ds(r0, C), :] + bias_ref[...]
        x = jnp.where(lane < MLSTM_GATE_LANE_F, gz, _log_sigmoid(gz))
        xt = _dot_nt(sel, x, HI)
        xc = _dot_nt(x, sel, HI)
        bt = _dot(xt, upper, HI)
        bc = _dot(tri, xc, HI)

        for h in range(H):
            sl = slice(h * Dh, (h + 1) * Dh)
            li_r = xt[h:h + 1, :]
            b_r = bt[H + h:H + h + 1, :]
            li_c = xc[:, h:h + 1]
            b_c = bc[:, H + h:H + h + 1]
            m_old = m_ref[h:h + 1, 0:1]
            log_d = jnp.where(incl, b_c - b_r + li_r, -jnp.inf)
            log_inter = b_c + m_old
            m_q = jnp.maximum(log_inter, jnp.max(log_d, axis=-1, keepdims=True))
            qh = q[:, sl].astype(BF16)
            kh = k[:, sl]
            s = _dot_nt(qh, kh.astype(BF16)) * jnp.exp(log_d - m_q)
            vaug = jnp.concatenate([v[:, sl].astype(BF16), ones_col], axis=1)
            c_old = c_ref[h]
            num = (jnp.exp(log_inter - m_q) * _dot(qh, c_old.astype(BF16))
                   + _dot(s.astype(BF16), vaug))
            den = num[:, Dh:Dh + 1]
            hout = num[:, :Dh] / jnp.maximum(jnp.abs(den), jnp.exp(-m_q))
            o_ref[0, pl.ds(r0, C), sl] = (_sigmoid(og[:, sl]) * hout).astype(o_ref.dtype)

            b_end = b_c[C - 1:C, :]
            m_new = jnp.maximum(b_end + m_old,
                                jnp.max(b_end - b_r + li_r, axis=-1, keepdims=True))
            gk = jnp.exp(b_end - b_c + li_c - m_new)
            dec = jnp.exp(b_end + m_old - m_new)
            c_ref[h] = dec * c_old + _dot_tn((gk * kh).astype(BF16), vaug)
            m_ref[h:h + 1, :] = jnp.broadcast_to(m_new, (1, LANES))
        return carry

    lax.fori_loop(0, nchunk, body, 0)


def _mlstm(zm, zs, bias, cwq, cwk, B, LP):
    W = zm.shape[2] // 4
    Dh = W // MLSTM_HEADS
    seq = lambda j: pl.BlockSpec((1, LP, W), lambda b: (b, 0, j))
    cw = pl.BlockSpec((SUBLANES, W), lambda b: (0, 0))
    return pl.pallas_call(
        _mlstm_kernel,
        grid=(B,),
        in_specs=[seq(0), seq(1), seq(2), seq(3),
                  pl.BlockSpec((1, LP, LANES), lambda b: (b, 0, 0)),
                  pl.BlockSpec((1, LANES), lambda b: (0, 0)), cw, cw],
        out_specs=pl.BlockSpec((1, LP, W), lambda b: (b, 0, 0)),
        out_shape=jax.ShapeDtypeStruct((B, LP, W), BF16),
        scratch_shapes=[pltpu.VMEM((MLSTM_HEADS, Dh, 2 * Dh), F32),
                        pltpu.VMEM((SUBLANES, LANES), F32)],
        compiler_params=_cparams(("parallel",)),
        name="mlstm",
    )(zm, zm, zm, zm, zs, bias, cwq, cwk)


def _pad_cols(w, n):
    return jnp.pad(w, ((0, 0), (0, n - w.shape[1])))


def _pad_rows(w, n):
    return jnp.pad(w, ((0, n - w.shape[0]), (0, 0)))


def kernel(x, meta_tokens, ln_emb_g, ln_emb_b, w_in, rwkv_mu, rwkv_w0, rwkv_w2, rwkv_a0, rwkv_a2, rwkv_g2, rwkv_k_k, rwkv_k_a, rwkv_r_k, rwkv_gn_g, rwkv_gn_b, fox_b_f, mlstm_conv_w, mlstm_b_i, mlstm_b_f, proj_rwkv, proj_fox, proj_mlstm, w_out, ln1_g, ln1_b, ffn_w1, ffn_w3, ffn_w2, router_w, moe_w1, moe_w3, moe_w2, ln2_g, ln2_b):
    B, S, D = x.shape
    L = N_META + S
    LP = -(-L // CHUNK) * CHUNK
    T = B * LP
    RW = proj_rwkv.shape[1]
    FW = proj_fox.shape[1]
    MW = proj_mlstm.shape[1]
    n_fox_heads = FW // FOX_HEAD_DIM
    dr, ar, gr = RWKV_DECAY_RANK, RWKV_AAA_RANK, RWKV_GATE_RANK
    c1 = 3 * RW + dr + ar + gr
    c2 = c1 + 3 * FW + n_fox_heads
    c3 = c2 + 4 * MW + 2 * MLSTM_HEADS

    meta = jnp.broadcast_to(meta_tokens[None].astype(x.dtype), (B, N_META, D))
    hin = jnp.concatenate([meta, x, jnp.zeros((B, LP - L, D), x.dtype)], axis=1).reshape(T, D)
    h, hb = _ln(hin, ln_emb_g, ln_emb_b)

    for l in range(DEPTH):
        wl = w_in[l]
        w_r = jnp.concatenate([
            _pad_cols(wl[:, 3 * RW:3 * RW + dr], LANES),
            _pad_cols(wl[:, 3 * RW + dr:3 * RW + dr + ar], LANES),
            wl[:, 3 * RW + dr + ar:c1], wl[:, :3 * RW]], axis=1).astype(BF16)
        w_f = wl[:, c1:c1 + 3 * FW].astype(BF16)
        w_m = wl[:, c2:c2 + 4 * MW].astype(BF16)
        w_s = _pad_cols(jnp.concatenate([wl[:, c1 + 3 * FW:c2], wl[:, c2 + 4 * MW:c3]], axis=1),
                        LANES).astype(BF16)
        w_g = wl[:, c3:].reshape(D, 3, D).transpose(1, 0, 2).astype(BF16)

        z_r = _mm(hb, w_r, F32, 1408, "in_rwkv").reshape(B, LP, -1)
        z_f = _mm(hb, w_f, F32, 1152, "in_fox").reshape(B, LP, -1)
        z_m = _mm(hb, w_m, F32, 1024, "in_mlstm").reshape(B, LP, -1)
        z_s = _mm(hb, w_s, F32, LANES, "in_gates").reshape(B, LP, LANES)

        mu = rwkv_mu[l]
        vecs = jnp.zeros((16, RW), F32)
        for i, p in enumerate([rwkv_w0[l], rwkv_a0[l], rwkv_k_k[l], rwkv_k_a[l],
                               rwkv_r_k[l].reshape(RW), rwkv_gn_g[l], rwkv_gn_b[l]]):
            vecs = vecs.at[i].set(p)
        vecs = vecs.at[8].set(mu[:RW]).at[9].set(mu[RW:2 * RW]).at[10].set(mu[2 * RW:3 * RW])
        mul = jnp.zeros((8, RWKV_LORA_W), F32)
        mul = mul.at[0, 0:dr].set(mu[3 * RW:3 * RW + dr])
        mul = mul.at[0, LANES:LANES + ar].set(mu[3 * RW + dr:3 * RW + dr + ar])
        mul = mul.at[0, 2 * LANES:2 * LANES + gr].set(mu[3 * RW + dr + ar:c1])
        y_r = _rwkv(z_r, vecs, mul, _pad_rows(rwkv_w2[l], LANES).astype(BF16),
                    _pad_rows(rwkv_a2[l], LANES).astype(BF16), rwkv_g2[l].astype(BF16), B, LP)

        bias = jnp.zeros((1, LANES), F32)
        bias = bias.at[0, 0:n_fox_heads].set(fox_b_f[l])
        bias = bias.at[0, MLSTM_GATE_LANE_I:MLSTM_GATE_LANE_I + MLSTM_HEADS].set(mlstm_b_i[l])
        bias = bias.at[0, MLSTM_GATE_LANE_F:MLSTM_GATE_LANE_F + MLSTM_HEADS].set(mlstm_b_f[l])

        ct = _fox_cum(z_s, bias, B, LP, FW // LANES)
        y_f = _fox(z_f, ct, B, LP, FW)

        cw = _pad_rows(mlstm_conv_w[l], SUBLANES)
        y_m = _mlstm(z_m, z_s, bias, cw[:, :MW], cw[:, MW:], B, LP)

        pre = _merge(hb, w_g, y_r.reshape(T, RW), y_f.reshape(T, FW), y_m.reshape(T, MW),
                     proj_rwkv[l].astype(BF16), proj_fox[l].astype(BF16),
                     proj_mlstm[l].astype(BF16))
        h, hb = _mm_res_ln(pre, w_out[l].astype(BF16), h, ln1_g[l], ln1_b[l])

        if l % 2 == 0:
            i = l // 2
            h, hb = _ffn(hb, ffn_w1[i][None].astype(BF16), ffn_w3[i][None].astype(BF16),
                         ffn_w2[i][None].astype(BF16), None, h, ln2_g[l], ln2_b[l])
        else:
            i = l // 2
            gates = _router(h, router_w[i])
            h, hb = _ffn(hb, moe_w1[i].astype(BF16), moe_w3[i].astype(BF16),
                         moe_w2[i].astype(BF16), gates, h, ln2_g[l], ln2_b[l])

    return h.reshape(B, LP, D)[:, N_META:L]
```

```python
import functools
import math

import jax
import jax.numpy as jnp
from jax import lax
from jax.experimental import pallas as pl
from jax.experimental.pallas import tpu as pltpu

F32 = jnp.float32
BF16 = jnp.bfloat16
HI = lax.Precision.HIGHEST

N_META = 16
DEPTH = 4
RWKV_HEAD_DIM = 64
RWKV_DECAY_RANK = 96
RWKV_AAA_RANK = 96
RWKV_GATE_RANK = 256
RWKV_GN_EPS = 64e-5
FOX_HEAD_DIM = 64
MLSTM_HEADS = 4
MLSTM_CONV = 4
N_EXPERTS = 8
LN_EPS = 1e-5
DEEPNORM_ALPHA = (2 * DEPTH) ** 0.25

CHUNK = 64
LANES = 128
SUBLANES = 8
ATT_BLOCK = 256
MOE_TILE = 512
VMEM_LIMIT = 56 * 1024 * 1024
NEG_BIG = -1e30


def _cparams(sem):
    return pltpu.CompilerParams(dimension_semantics=sem, vmem_limit_bytes=VMEM_LIMIT)


def _tile(n, pref, mult=SUBLANES):
    best = None
    for d in range(mult, min(n, pref) + 1, mult):
        if n % d == 0:
            best = d
    return best if best is not None else n


def _dot(a, b, precision=None):
    return jnp.dot(a, b, preferred_element_type=F32, precision=precision)


def _dot_nt(a, b, precision=None):
    return lax.dot_general(a, b, (((1,), (1,)), ((), ())), preferred_element_type=F32,
                           precision=precision)


def _dot_tn(a, b, precision=None):
    return lax.dot_general(a, b, (((0,), (0,)), ((), ())), preferred_element_type=F32,
                           precision=precision)


def _softplus(x):
    return jnp.maximum(x, 0.0) + jnp.log(1.0 + jnp.exp(-jnp.abs(x)))


def _log_sigmoid(x):
    return jnp.minimum(x, 0.0) - jnp.log(1.0 + jnp.exp(-jnp.abs(x)))


def _sigmoid(x):
    return 1.0 / (1.0 + jnp.exp(-x))


def _iota(shape, dim):
    return lax.broadcasted_iota(jnp.int32, shape, dim)


def _layer_norm(t, g, b):
    mu = jnp.mean(t, axis=-1, keepdims=True)
    d = t - mu
    var = jnp.mean(d * d, axis=-1, keepdims=True)
    return d * lax.rsqrt(var + LN_EPS) * g + b


def _ln_kernel(x_ref, g_ref, b_ref, o_ref, ob_ref):
    y = _layer_norm(x_ref[...], g_ref[...], b_ref[...])
    o_ref[...] = y
    ob_ref[...] = y.astype(BF16)


def _ln(x, g, b):
    T, D = x.shape
    bm = _tile(T, 512)
    return pl.pallas_call(
        _ln_kernel,
        grid=(T // bm,),
        in_specs=[pl.BlockSpec((bm, D), lambda i: (i, 0)),
                  pl.BlockSpec((1, D), lambda i: (0, 0)),
                  pl.BlockSpec((1, D), lambda i: (0, 0))],
        out_specs=[pl.BlockSpec((bm, D), lambda i: (i, 0)),
                   pl.BlockSpec((bm, D), lambda i: (i, 0))],
        out_shape=[jax.ShapeDtypeStruct((T, D), F32), jax.ShapeDtypeStruct((T, D), BF16)],
        compiler_params=_cparams(("parallel",)),
        name="ln_embed",
    )(x, g.reshape(1, D), b.reshape(1, D))


def _mm_kernel(a_ref, b_ref, o_ref):
    o_ref[...] = _dot(a_ref[...], b_ref[...]).astype(o_ref.dtype)


def _mm(a, b, out_dtype, bn_pref, name):
    M, K = a.shape
    N = b.shape[1]
    bm = _tile(M, 1024)
    bn = _tile(N, bn_pref, LANES)
    return pl.pallas_call(
        _mm_kernel,
        grid=(M // bm, N // bn),
        in_specs=[pl.BlockSpec((bm, K), lambda i, j: (i, 0)),
                  pl.BlockSpec((K, bn), lambda i, j: (0, j))],
        out_specs=pl.BlockSpec((bm, bn), lambda i, j: (i, j)),
        out_shape=jax.ShapeDtypeStruct((M, N), out_dtype),
        compiler_params=_cparams(("parallel", "arbitrary")),
        name=name,
    )(a, b)


def _mm_res_ln_kernel(a_ref, w_ref, h_ref, g_ref, b_ref, o_ref, ob_ref):
    t = DEEPNORM_ALPHA * h_ref[...] + _dot(a_ref[...], w_ref[...])
    y = _layer_norm(t, g_ref[...], b_ref[...])
    o_ref[...] = y
    ob_ref[...] = y.astype(BF16)


def _mm_res_ln(a, w, h, g, b):
    M, K = a.shape
    D = w.shape[1]
    bm = _tile(M, 512)
    return pl.pallas_call(
        _mm_res_ln_kernel,
        grid=(M // bm,),
        in_specs=[pl.BlockSpec((bm, K), lambda i: (i, 0)),
                  pl.BlockSpec((K, D), lambda i: (0, 0)),
                  pl.BlockSpec((bm, D), lambda i: (i, 0)),
                  pl.BlockSpec((1, D), lambda i: (0, 0)),
                  pl.BlockSpec((1, D), lambda i: (0, 0))],
        out_specs=[pl.BlockSpec((bm, D), lambda i: (i, 0)),
                   pl.BlockSpec((bm, D), lambda i: (i, 0))],
        out_shape=[jax.ShapeDtypeStruct((M, D), F32), jax.ShapeDtypeStruct((M, D), BF16)],
        compiler_params=_cparams(("parallel",)),
        name="out_proj_ln",
    )(a, w, h, g.reshape(1, D), b.reshape(1, D))


def _merge_kernel(hb_ref, wgr_ref, wgf_ref, wgm_ref, yr_ref, yf_ref, ym_ref, pr_ref, pf_ref,
                  pm_ref, o_ref):
    hb = hb_ref[...]
    acc = _sigmoid(_dot(hb, wgr_ref[...])) * _dot(yr_ref[...], pr_ref[...])
    acc += _sigmoid(_dot(hb, wgf_ref[...])) * _dot(yf_ref[...], pf_ref[...])
    acc += _sigmoid(_dot(hb, wgm_ref[...])) * _dot(ym_ref[...], pm_ref[...])
    o_ref[...] = acc.astype(o_ref.dtype)


def _merge(hb, wg, yr, yf, ym, pr, pf, pm):
    M, D = hb.shape
    bm = _tile(M, 1024)
    bn = _tile(D, 512, LANES)
    nb = D // bn
    row = lambda w: pl.BlockSpec((bm, w), lambda i, j: (i, 0))
    col = lambda k: pl.BlockSpec((k, bn), lambda i, j: (0, j))
    gate = lambda br: pl.BlockSpec((D, bn), lambda i, j: (0, br * nb + j))
    return pl.pallas_call(
        _merge_kernel,
        grid=(M // bm, nb),
        in_specs=[row(D), gate(0), gate(1), gate(2),
                  row(yr.shape[1]), row(yf.shape[1]), row(ym.shape[1]),
                  col(pr.shape[0]), col(pf.shape[0]), col(pm.shape[0])],
        out_specs=pl.BlockSpec((bm, bn), lambda i, j: (i, j)),
        out_shape=jax.ShapeDtypeStruct((M, D), BF16),
        compiler_params=_cparams(("parallel", "arbitrary")),
        name="merge",
    )(hb, wg, wg, wg, yr, yf, ym, pr, pf, pm)


def _router_kernel(h_ref, w_ref, ow_ref, oi_ref):
    logits = _dot(h_ref[...], w_ref[...], HI)
    lane = _iota(logits.shape, 1)
    logits = jnp.where(lane < N_EXPERTS, logits, -jnp.inf)
    m1 = jnp.max(logits, axis=-1, keepdims=True)
    i1 = jnp.min(jnp.where(logits == m1, lane, LANES), axis=-1, keepdims=True)
    rest = jnp.where(lane == i1, -jnp.inf, logits)
    m2 = jnp.max(rest, axis=-1, keepdims=True)
    i2 = jnp.min(jnp.where(rest == m2, lane, LANES), axis=-1, keepdims=True)
    e2 = jnp.exp(m2 - m1)
    den = 1.0 + e2
    ow_ref[...] = jnp.where(lane == 0, 1.0 / den, jnp.where(lane == 1, e2 / den, 0.0))
    oi_ref[...] = jnp.where(lane == 0, i1, jnp.where(lane == 1, i2, 0))


def _router(h, w):
    T, D = h.shape
    bm = _tile(T, 512)
    wp = jnp.zeros((D, LANES), F32).at[:, :N_EXPERTS].set(w)
    out = pl.BlockSpec((bm, LANES), lambda i: (i, 0))
    return pl.pallas_call(
        _router_kernel,
        grid=(T // bm,),
        in_specs=[pl.BlockSpec((bm, D), lambda i: (i, 0)),
                  pl.BlockSpec((D, LANES), lambda i: (0, 0))],
        out_specs=[out, out],
        out_shape=[jax.ShapeDtypeStruct((T, LANES), F32),
                   jax.ShapeDtypeStruct((T, LANES), jnp.int32)],
        compiler_params=_cparams(("parallel",)),
        name="router",
    )(h, wp)


def _route(ids, tm, n_tiles):
    T = ids.shape[0]
    e = ids[:, :2].reshape(-1)
    onehot = (e[:, None] == jnp.arange(N_EXPERTS, dtype=jnp.int32)[None, :]).astype(jnp.int32)
    pos = jnp.take_along_axis(jnp.cumsum(onehot, axis=0), e[:, None], axis=1)[:, 0] - 1
    counts = jnp.sum(onehot, axis=0)
    padded = (counts + tm - 1) // tm * tm
    ends = jnp.cumsum(padded)
    dest = (ends - padded)[e] + pos
    row_token = jnp.zeros((n_tiles * tm,), jnp.int32).at[dest].set(
        jnp.arange(2 * T, dtype=jnp.int32) // 2)
    tile_expert = jnp.minimum(
        jnp.searchsorted(ends, jnp.arange(n_tiles, dtype=jnp.int32) * tm, side="right"),
        N_EXPERTS - 1).astype(jnp.int32)
    n_used = (ends[-1] // tm).astype(jnp.int32).reshape(1)
    return row_token.reshape(n_tiles, 1, tm), dest, tile_expert, n_used


def _row_copy(src_ref, buf_ref, sem, src_row, dst_row):
    return pltpu.make_async_copy(src_ref.at[pl.ds(src_row, 1), :],
                                 buf_ref.at[pl.ds(dst_row, 1), :], sem.at[0])


def _gather_kernel(idx_ref, src_ref, o_ref, buf_ref, sem):
    tm = buf_ref.shape[0]

    def issue(r, c):
        _row_copy(src_ref, buf_ref, sem, idx_ref[0, 0, r], r).start()
        return c

    def drain(r, c):
        _row_copy(src_ref, buf_ref, sem, 0, r).wait()
        return c

    lax.fori_loop(0, tm, issue, 0, unroll=8)
    lax.fori_loop(0, tm, drain, 0, unroll=8)
    o_ref[...] = buf_ref[...].astype(o_ref.dtype)


def _gather_rows(src, row_token):
    n_tiles, _, tm = row_token.shape
    D = src.shape[1]
    return pl.pallas_call(
        _gather_kernel,
        grid=(n_tiles,),
        in_specs=[pl.BlockSpec((1, 1, tm), lambda i: (i, 0, 0), memory_space=pltpu.SMEM),
                  pl.BlockSpec(memory_space=pl.ANY)],
        out_specs=pl.BlockSpec((tm, D), lambda i: (i, 0)),
        out_shape=jax.ShapeDtypeStruct((n_tiles * tm, D), BF16),
        scratch_shapes=[pltpu.VMEM((tm, D), F32), pltpu.SemaphoreType.DMA((1,))],
        compiler_params=_cparams(("arbitrary",)),
        name="moe_gather",
    )(row_token, src)


def _moe_ffn_kernel(te_ref, nu_ref, x_ref, w1_ref, w3_ref, w2_ref, o_ref):
    j = pl.program_id(1)

    @pl.when(j == 0)
    def _():
        o_ref[...] = jnp.zeros_like(o_ref)

    @pl.when(pl.program_id(0) < nu_ref[0])
    def _():
        x = x_ref[...]
        a = _dot(x, w1_ref[0])
        c = _dot(x, w3_ref[0])
        o_ref[...] += _dot((a * _sigmoid(a) * c).astype(BF16), w2_ref[0])


def _moe_ffn(xs, w1, w3, w2, tile_expert, n_used, tm):
    NP, D = xs.shape
    FF = w1.shape[2]
    bf = _tile(FF, 512, LANES)
    grid_spec = pltpu.PrefetchScalarGridSpec(
        num_scalar_prefetch=2,
        grid=(NP // tm, FF // bf),
        in_specs=[pl.BlockSpec((tm, D), lambda i, j, te, nu: (i, 0)),
                  pl.BlockSpec((1, D, bf), lambda i, j, te, nu: (te[i], 0, j)),
                  pl.BlockSpec((1, D, bf), lambda i, j, te, nu: (te[i], 0, j)),
                  pl.BlockSpec((1, bf, D), lambda i, j, te, nu: (te[i], j, 0))],
        out_specs=pl.BlockSpec((tm, D), lambda i, j, te, nu: (i, 0)),
    )
    return pl.pallas_call(
        _moe_ffn_kernel,
        grid_spec=grid_spec,
        out_shape=jax.ShapeDtypeStruct((NP, D), F32),
        compiler_params=_cparams(("parallel", "arbitrary")),
        name="moe_ffn",
    )(tile_expert, n_used, xs, w1, w3, w2)


def _combine_kernel(d_ref, y_ref, gw_ref, h_ref, g_ref, b_ref, o_ref, ob_ref, buf_ref, sem):
    bm = buf_ref.shape[1]

    def issue(t, c):
        for r in range(2):
            _row_copy(y_ref, buf_ref.at[r], sem, d_ref[0, 0, 2 * t + r], t).start()
        return c

    def drain(t, c):
        for r in range(2):
            _row_copy(y_ref, buf_ref.at[r], sem, 0, t).wait()
        return c

    lax.fori_loop(0, bm, issue, 0, unroll=4)
    lax.fori_loop(0, bm, drain, 0, unroll=4)
    gw = gw_ref[...]
    ff = gw[:, 0:1] * buf_ref[0] + gw[:, 1:2] * buf_ref[1]
    y = _layer_norm(DEEPNORM_ALPHA * h_ref[...] + ff, g_ref[...], b_ref[...])
    o_ref[...] = y
    ob_ref[...] = y.astype(BF16)


def _combine(ys, dest, gw, h, g, b):
    T, D = h.shape
    bm = _tile(T, 256)
    row = lambda w: pl.BlockSpec((bm, w), lambda i: (i, 0))
    vec = pl.BlockSpec((1, D), lambda i: (0, 0))
    return pl.pallas_call(
        _combine_kernel,
        grid=(T // bm,),
        in_specs=[pl.BlockSpec((1, 1, 2 * bm), lambda i: (i, 0, 0), memory_space=pltpu.SMEM),
                  pl.BlockSpec(memory_space=pl.ANY), row(LANES), row(D), vec, vec],
        out_specs=[row(D), row(D)],
        out_shape=[jax.ShapeDtypeStruct((T, D), F32), jax.ShapeDtypeStruct((T, D), BF16)],
        scratch_shapes=[pltpu.VMEM((2, bm, D), F32), pltpu.SemaphoreType.DMA((1,))],
        compiler_params=_cparams(("arbitrary",)),
        name="moe_combine",
    )(dest.reshape(T // bm, 1, 2 * bm), ys, gw, h, g.reshape(1, D), b.reshape(1, D))


def _ffn_kernel(x_ref, w1_ref, w3_ref, w2_ref, h_ref, g_ref, b_ref, o_ref, ob_ref, acc_ref):
    j = pl.program_id(1)

    @pl.when(j == 0)
    def _():
        acc_ref[...] = jnp.zeros_like(acc_ref)

    x = x_ref[...]
    a = _dot(x, w1_ref[...])
    c = _dot(x, w3_ref[...])
    acc_ref[...] += _dot((a * _sigmoid(a) * c).astype(BF16), w2_ref[...])

    @pl.when(j == pl.num_programs(1) - 1)
    def _():
        y = _layer_norm(DEEPNORM_ALPHA * h_ref[...] + acc_ref[...], g_ref[...], b_ref[...])
        o_ref[...] = y
        ob_ref[...] = y.astype(BF16)


def _ffn(xb, w1, w3, w2, h, g, b):
    M, D = xb.shape
    FF = w1.shape[1]
    bm = _tile(M, 512)
    bf = _tile(FF, 512, LANES)
    row = lambda w: pl.BlockSpec((bm, w), lambda i, j: (i, 0))
    vec = pl.BlockSpec((1, D), lambda i, j: (0, 0))
    return pl.pallas_call(
        _ffn_kernel,
        grid=(M // bm, FF // bf),
        in_specs=[row(D),
                  pl.BlockSpec((D, bf), lambda i, j: (0, j)),
                  pl.BlockSpec((D, bf), lambda i, j: (0, j)),
                  pl.BlockSpec((bf, D), lambda i, j: (j, 0)),
                  row(D), vec, vec],
        out_specs=[row(D), row(D)],
        out_shape=[jax.ShapeDtypeStruct((M, D), F32), jax.ShapeDtypeStruct((M, D), BF16)],
        scratch_shapes=[pltpu.VMEM((bm, D), F32)],
        compiler_params=_cparams(("parallel", "arbitrary")),
        name="ffn",
    )(xb, w1, w3, w2, h, g.reshape(1, D), b.reshape(1, D))


RWKV_LORA_W = 512
RWKV_SUM_W = 256
BF16_ROWS = 16


def _shift_lerp(ref, c0, width, r0, p0, notfirst, mu, row0):
    cur = ref[0, pl.ds(r0, CHUNK), c0:c0 + width].astype(F32)
    prev = ref[0, pl.ds(p0, BF16_ROWS), c0:c0 + width][BF16_ROWS - 1:BF16_ROWS, :].astype(F32)
    zprev = jnp.where(row0, prev * notfirst, pltpu.roll(cur, 1, 0))
    return cur + mu * (zprev - cur)


def _split3(x):
    hi = x.astype(BF16)
    r1 = x - hi.astype(F32)
    mid = r1.astype(BF16)
    lo = (r1 - mid.astype(F32)).astype(BF16)
    return hi, mid, lo


def _rwkv_kernel(z_ref, vec_ref, mul_ref, w2_ref, a2_ref, g2_ref, o_ref, s_ref):
    C, N = CHUNK, RWKV_HEAD_DIM
    W = o_ref.shape[2]
    H = W // N
    LW = RWKV_LORA_W
    nchunk = z_ref.shape[1] // C
    s_ref[...] = jnp.zeros_like(s_ref)

    ri = _iota((C, C), 0)
    ci = _iota((C, C), 1)
    strict = ri > ci
    incl = ri >= ci
    tri = incl.astype(BF16)
    eye = (ri == ci).astype(F32)
    SW = RWKV_SUM_W
    bd = (_iota((SW, SW), 0) // N == _iota((SW, SW), 1) // N).astype(BF16)
    row0_w = _iota((C, W), 0) == 0
    row0_l = _iota((C, LW), 0) == 0

    def head_sum(x):
        xb = x.astype(BF16)
        return jnp.concatenate([_dot(xb[:, g:g + SW], bd) for g in range(0, W, SW)], axis=1)

    w0 = vec_ref[0:1, :]
    a0 = vec_ref[1:2, :]
    k_k = vec_ref[2:3, :]
    k_a = vec_ref[3:4, :]
    r_k = vec_ref[4:5, :]
    gn_g = vec_ref[5:6, :]
    gn_b = vec_ref[6:7, :]
    mu_r = vec_ref[8:9, :]
    mu_k = vec_ref[9:10, :]
    mu_v = vec_ref[10:11, :]
    mu_l = mul_ref[0:1, :]

    def body(c, carry):
        r0 = pl.multiple_of(c * C, C)
        p0 = pl.multiple_of(jnp.maximum(r0 - BF16_ROWS, 0), BF16_ROWS)
        notfirst = jnp.where(c > 0, 1.0, 0.0).astype(F32)
        lo = _shift_lerp(z_ref, 0, LW, r0, p0, notfirst, mu_l, row0_l)
        r = _shift_lerp(z_ref, LW, W, r0, p0, notfirst, mu_r, row0_w)
        k = _shift_lerp(z_ref, LW + W, W, r0, p0, notfirst, mu_k, row0_w)
        v = _shift_lerp(z_ref, LW + 2 * W, W, r0, p0, notfirst, mu_v, row0_w)

        wd = lo[:, 0:128]
        ad = lo[:, 128:256]
        gd = lo[:, 256:512]
        w_log = -_softplus(-(w0 + _dot(jnp.tanh(wd).astype(BF16), w2_ref[...]))) - 0.5
        lw = -jnp.exp(w_log)
        a = _sigmoid(a0 + _dot(ad.astype(BF16), a2_ref[...]))
        g = _dot(_sigmoid(gd).astype(BF16), g2_ref[...])

        kk = k * k_k
        kk = kk / jnp.maximum(jnp.sqrt(head_sum(kk * kk)), 1e-12)
        kmod = k * (1.0 + (a - 1.0) * k_a)
        b = kk * a

        cum = sum(_dot(tri, part) for part in _split3(lw))
        cum_c = cum[C - 1:C, :]
        e_inv = jnp.exp(-cum)
        e_fin = jnp.exp(cum_c - cum)
        kkt = kk * jnp.exp(cum - lw)
        rt = r * jnp.exp(cum)
        bt = b * e_inv
        kt = kmod * e_inv
        bh = b * e_fin
        kh = kmod * e_fin
        pc = jnp.exp(cum_c)

        hs = range(H)
        sl = [slice(h * N, (h + 1) * N) for h in hs]
        cat = lambda top, bot, h: jnp.concatenate([top[:, sl[h]], bot[:, sl[h]]], axis=0).astype(BF16)
        lhs = [cat(kkt, rt, h) for h in hs]
        rhs = [cat(bt, kt, h) for h in hs]
        vb = [v[:, sl[h]].astype(BF16) for h in hs]
        s_old = [s_ref[h] for h in hs]
        mall = [_dot_nt(lhs[h], rhs[h]) for h in hs]
        gm = [_dot_nt(lhs[h], s_old[h].astype(BF16)) for h in hs]
        mb = [jnp.where(incl, mall[h][C:, :C], 0.0).astype(BF16) for h in hs]
        akmk = [jnp.concatenate([jnp.where(strict, mall[h][:C, C:], 0.0),
                                 jnp.where(incl, mall[h][C:, C:], 0.0)], axis=0).astype(BF16)
                for h in hs]
        av = [_dot(akmk[h], vb[h]) for h in hs]
        x = [jnp.where(strict, -mall[h][:C, :C], 0.0) for h in hs]
        t = [eye + x[h] for h in hs]
        for _ in range(5):
            xb = [x[h].astype(BF16) for h in hs]
            x = [_dot(xb[h], xb[h]) for h in hs]
            t = [t[h] + _dot(t[h].astype(BF16), x[h].astype(BF16)) for h in hs]
        u = [_dot(t[h].astype(BF16), (gm[h][:C] + av[h][:C]).astype(BF16)) for h in hs]
        o = [gm[h][C:] + av[h][C:] - _dot(mb[h], u[h].astype(BF16)) for h in hs]
        upd = [_dot_tn(jnp.concatenate([u[h].astype(BF16), vb[h]], axis=0), cat(-bh, kh, h))
               for h in hs]
        for h in hs:
            s_ref[h] = s_old[h] * pc[:, sl[h]] + upd[h]

        o = jnp.concatenate(o, axis=1)
        d = o - head_sum(o) * (1.0 / N)
        var = head_sum(d * d) * (1.0 / N)
        on = d * lax.rsqrt(var + RWKV_GN_EPS) * gn_g + gn_b
        bonus = head_sum(r * kmod * r_k) * v
        o_ref[0, pl.ds(r0, C), :] = ((on + bonus) * g).astype(o_ref.dtype)
        return carry

    lax.fori_loop(0, nchunk, body, 0)


def _rwkv(z, vecs, mul, w2, a2, g2, B, LP):
    W = vecs.shape[1]
    ZW = z.shape[2]
    whole = lambda shape: pl.BlockSpec(shape, lambda b: (0,) * len(shape))
    return pl.pallas_call(
        _rwkv_kernel,
        grid=(B,),
        in_specs=[pl.BlockSpec((1, LP, ZW), lambda b: (b, 0, 0)),
                  whole((16, W)), whole((8, RWKV_LORA_W)), whole((LANES, W)), whole((LANES, W)),
                  whole((RWKV_GATE_RANK, W))],
        out_specs=pl.BlockSpec((1, LP, W), lambda b: (b, 0, 0)),
        out_shape=jax.ShapeDtypeStruct((B, LP, W), BF16),
        scratch_shapes=[pltpu.VMEM((W // RWKV_HEAD_DIM, RWKV_HEAD_DIM, RWKV_HEAD_DIM), F32)],
        compiler_params=_cparams(("parallel",)),
        name="rwkv7",
    )(z, vecs, mul, w2, a2, g2)


def _fox_cum_kernel(zs_ref, bias_ref, o_ref):
    LP = zs_ref.shape[1]
    nkb = o_ref.shape[2]
    TB = ATT_BLOCK
    o_ref[...] = jnp.zeros_like(o_ref)
    sel = (_iota((16, LANES), 0) == _iota((16, LANES), 1)).astype(F32)
    carry = jnp.zeros((16, 1), F32)
    for kb in range(nkb):
        n = min(TB, LP - kb * TB)
        lf = _log_sigmoid(zs_ref[0, kb * TB:kb * TB + n, :] + bias_ref[...])
        lft = _dot_nt(sel, lf, HI)
        upper = (_iota((n, n), 0) <= _iota((n, n), 1)).astype(F32)
        cs = _dot(lft, upper, HI) + carry
        carry = cs[:, n - 1:n]
        for hp in range(o_ref.shape[1]):
            o_ref[0, hp, kb, 0:2, 0:n] = cs[2 * hp:2 * hp + 2, :]


def _fox_cum(zs, bias, B, LP, n_pairs):
    nkb = pl.cdiv(LP, ATT_BLOCK)
    return pl.pallas_call(
        _fox_cum_kernel,
        grid=(B,),
        in_specs=[pl.BlockSpec((1, LP, LANES), lambda b: (b, 0, 0)),
                  pl.BlockSpec((1, LANES), lambda b: (0, 0))],
        out_specs=pl.BlockSpec((1, n_pairs, nkb, SUBLANES, ATT_BLOCK), lambda b: (b, 0, 0, 0, 0)),
        out_shape=jax.ShapeDtypeStruct((B, n_pairs, nkb, SUBLANES, ATT_BLOCK), F32),
        compiler_params=_cparams(("parallel",)),
        name="fox_cum",
    )(zs, bias)


def _fox_kernel(q_ref, k_ref, v_ref, ct_ref, o_ref):
    TB, N = ATT_BLOCK, FOX_HEAD_DIM
    LP = k_ref.shape[1]
    n_full = LP // TB
    tail = LP - n_full * TB
    qi = pl.program_id(2)
    HQ = TB // 2
    units = [(h, s) for h in range(2) for s in range(2)]
    lane_q = _iota((HQ, LANES), 1)
    q = q_ref[0] * (N ** -0.5)
    qu = [jnp.where((lane_q < N) == (h == 0), q[s * HQ:(s + 1) * HQ], 0.0) for h, s in units]

    def block(kblk, vblk, cks, masks, state):
        n = kblk.shape[0]
        lane_v = _iota((n, LANES), 1)
        vaug = [jnp.where((lane_v < N) == (h == 0), vblk, 1.0) for h in range(2)]
        sc = [_dot_nt(qu[i], kblk) - cks[h] for i, (h, s) in enumerate(units)]
        if masks is not None:
            sc = [jnp.where(masks[s], sc[i], -jnp.inf) for i, (h, s) in enumerate(units)]
        m_new = [jnp.maximum(state[i][0], jnp.max(sc[i], axis=-1, keepdims=True))
                 for i in range(4)]
        p = [jnp.exp(sc[i] - m_new[i]).astype(BF16) for i in range(4)]
        pv = [_dot(p[i], vaug[h]) for i, (h, s) in enumerate(units)]
        return tuple((m_new[i], jnp.exp(state[i][0] - m_new[i]) * state[i][1] + pv[i])
                     for i in range(4))

    def full_body(kb, state):
        k0 = pl.multiple_of(kb * TB, TB)
        cks = [ct_ref[0, 0, kb, h:h + 1, :] for h in range(2)]
        return block(k_ref[0, pl.ds(k0, TB), :], v_ref[0, pl.ds(k0, TB), :], cks, None, state)

    init = tuple((jnp.full((HQ, 1), NEG_BIG, F32), jnp.zeros((HQ, LANES), F32)) for _ in units)
    state = lax.fori_loop(0, qi, full_body, init)

    def finish(n):
        k0 = pl.multiple_of(qi * TB, TB)
        cks = [ct_ref[0, 0, qi, h:h + 1, :][:, 0:n] for h in range(2)]
        masks = [_iota((HQ, n), 0) + s * HQ >= _iota((HQ, n), 1) for s in range(2)]
        final = block(k_ref[0, pl.ds(k0, n), :], v_ref[0, pl.ds(k0, n), :], cks, masks, state)
        for s in range(2):
            acc0, acc1 = final[s][1], final[2 + s][1]
            out = jnp.where(lane_q < N, acc0 / acc0[:, N:N + 1], acc1 / acc1[:, 0:1])
            o_ref[0, s * HQ:(s + 1) * HQ, :] = out.astype(o_ref.dtype)

    if tail == 0:
        finish(TB)
    else:
        pl.when(qi < n_full)(lambda: finish(TB))
        pl.when(qi == n_full)(lambda: finish(tail))


def _fox(zq, ct, B, LP, width):
    n_pairs = width // LANES
    nq = pl.cdiv(LP, ATT_BLOCK)
    nkb = ct.shape[2]
    kv = lambda off: pl.BlockSpec((1, LP, LANES), lambda b, p, i: (b, 0, off + p))
    return pl.pallas_call(
        _fox_kernel,
        grid=(B, n_pairs, nq),
        in_specs=[pl.BlockSpec((1, ATT_BLOCK, LANES), lambda b, p, i: (b, i, p)),
                  kv(n_pairs), kv(2 * n_pairs),
                  pl.BlockSpec((1, 1, nkb, SUBLANES, ATT_BLOCK), lambda b, p, i: (b, p, 0, 0, 0))],
        out_specs=pl.BlockSpec((1, ATT_BLOCK, LANES), lambda b, p, i: (b, i, p)),
        out_shape=jax.ShapeDtypeStruct((B, LP, width), BF16),
        compiler_params=_cparams(("parallel", "parallel", "arbitrary")),
        name="fox_attention",
    )(zq, zq, zq, ct)


MLSTM_GATE_LANE_I = 12
MLSTM_GATE_LANE_F = 16


def _conv_silu(ref, r0, p0, notfirst, w_ref):
    cur = ref[0, pl.ds(r0, CHUNK), :].astype(F32)
    prev = ref[0, pl.ds(p0, BF16_ROWS), :][BF16_ROWS - SUBLANES:, :].astype(F32) * notfirst
    row8 = _iota(prev.shape, 0)
    y = cur * w_ref[MLSTM_CONV - 1:MLSTM_CONV, :]
    for s in range(1, MLSTM_CONV):
        rolled = pltpu.roll(cur, s, 0)
        top = jnp.where(row8 < s, pltpu.roll(prev, s, 0), rolled[0:SUBLANES])
        shifted = jnp.concatenate([top, rolled[SUBLANES:]], axis=0)
        y = y + shifted * w_ref[MLSTM_CONV - 1 - s:MLSTM_CONV - s, :]
    return y * _sigmoid(y)


def _mlstm_kernel(q_ref, k_ref, v_ref, og_ref, zs_ref, bias_ref, cwq_ref, cwk_ref,
                  o_ref, c_ref, m_ref):
    C, H = CHUNK, MLSTM_HEADS
    Dh = q_ref.shape[2] // H
    nchunk = q_ref.shape[1] // C
    c_ref[...] = jnp.zeros_like(c_ref)
    m_ref[...] = jnp.zeros_like(m_ref)

    ri = _iota((C, C), 0)
    ci = _iota((C, C), 1)
    incl = ri >= ci
    tri = incl.astype(F32)
    upper = (ri <= ci).astype(F32)
    sel = (_iota((SUBLANES, LANES), 1) == _iota((SUBLANES, LANES), 0) + MLSTM_GATE_LANE_I).astype(F32)
    ones_col = (_iota((C, LANES), 1) == 0).astype(BF16)
    lane = _iota((C, LANES), 1)

    def body(c, carry):
        r0 = pl.multiple_of(c * C, C)
        p0 = pl.multiple_of(jnp.maximum(r0 - BF16_ROWS, 0), BF16_ROWS)
        notfirst = jnp.where(c > 0, 1.0, 0.0).astype(F32)
        q = _conv_silu(q_ref, r0, p0, notfirst, cwq_ref)
        k = _conv_silu(k_ref, r0, p0, notfirst, cwk_ref) * (Dh ** -0.5)
        v = v_ref[0, pl.ds(r0, C), :]
        og = og_ref[0, pl.ds(r0, C), :].astype(F32)
        gz = zs_ref[0, pl.ds(r0, C), :] + bias_ref[...]
        x = jnp.where(lane < MLSTM_GATE_LANE_F, gz, _log_sigmoid(gz))
        xt = _dot_nt(sel, x, HI)
        xc = _dot_nt(x, sel, HI)
        bt = _dot(xt, upper, HI)
        bc = _dot(tri, xc, HI)

        hs = range(H)
        sl = [slice(h * Dh, (h + 1) * Dh) for h in hs]
        m_all = m_ref[...]
        m_old = [m_all[h:h + 1, 0:1] for h in hs]
        c_old = [c_ref[h] for h in hs]
        li_r = [xt[h:h + 1, :] for h in hs]
        b_r = [bt[H + h:H + h + 1, :] for h in hs]
        li_c = [xc[:, h:h + 1] for h in hs]
        b_c = [bc[:, H + h:H + h + 1] for h in hs]
        qh = [q[:, sl[h]].astype(BF16) for h in hs]
        kh = [k[:, sl[h]] for h in hs]
        vaug = [jnp.concatenate([v[:, sl[h]], ones_col], axis=1) for h in hs]
        qk = [_dot_nt(qh[h], kh[h].astype(BF16)) for h in hs]
        qc = [_dot(qh[h], c_old[h].astype(BF16)) for h in hs]
        log_d = [jnp.where(incl, b_c[h] - b_r[h] + li_r[h], -jnp.inf) for h in hs]
        log_inter = [b_c[h] + m_old[h] for h in hs]
        m_q = [jnp.maximum(log_inter[h], jnp.max(log_d[h], axis=-1, keepdims=True)) for h in hs]
        s = [(qk[h] * jnp.exp(log_d[h] - m_q[h])).astype(BF16) for h in hs]
        b_end = [b_c[h][C - 1:C, :] for h in hs]
        m_new = [jnp.maximum(b_end[h] + m_old[h],
                             jnp.max(b_end[h] - b_r[h] + li_r[h], axis=-1, keepdims=True))
                 for h in hs]
        gk = [jnp.exp(b_end[h] - b_c[h] + li_c[h] - m_new[h]) for h in hs]
        num = [jnp.exp(log_inter[h] - m_q[h]) * qc[h] + _dot(s[h], vaug[h]) for h in hs]
        kv = [_dot_tn((gk[h] * kh[h]).astype(BF16), vaug[h]) for h in hs]
        outs = [_sigmoid(og[:, sl[h]]) * num[h][:, :Dh]
                / jnp.maximum(jnp.abs(num[h][:, Dh:Dh + 1]), jnp.exp(-m_q[h])) for h in hs]
        o_ref[0, pl.ds(r0, C), :] = jnp.concatenate(outs, axis=1).astype(o_ref.dtype)
        for h in hs:
            c_ref[h] = jnp.exp(b_end[h] + m_old[h] - m_new[h]) * c_old[h] + kv[h]
        m_ref[0:H, :] = jnp.concatenate([jnp.broadcast_to(m_new[h], (1, LANES)) for h in hs],
                                        axis=0)
        return carry

    lax.fori_loop(0, nchunk, body, 0)


def _mlstm(zm, zs, bias, cwq, cwk, B, LP):
    W = zm.shape[2] // 4
    Dh = W // MLSTM_HEADS
    seq = lambda j: pl.BlockSpec((1, LP, W), lambda b: (b, 0, j))
    cw = pl.BlockSpec((SUBLANES, W), lambda b: (0, 0))
    return pl.pallas_call(
        _mlstm_kernel,
        grid=(B,),
        in_specs=[seq(0), seq(1), seq(2), seq(3),
                  pl.BlockSpec((1, LP, LANES), lambda b: (b, 0, 0)),
                  pl.BlockSpec((1, LANES), lambda b: (0, 0)), cw, cw],
        out_specs=pl.BlockSpec((1, LP, W), lambda b: (b, 0, 0)),
        out_shape=jax.ShapeDtypeStruct((B, LP, W), BF16),
        scratch_shapes=[pltpu.VMEM((MLSTM_HEADS, Dh, 2 * Dh), F32),
                        pltpu.VMEM((SUBLANES, LANES), F32)],
        compiler_params=_cparams(("parallel",)),
        name="mlstm",
    )(zm, zm, zm, zm, zs, bias, cwq, cwk)


def _pad_cols(w, n):
    return jnp.pad(w, ((0, 0), (0, n - w.shape[1])))


def _pad_rows(w, n):
    return jnp.pad(w, ((0, n - w.shape[0]), (0, 0)))


def kernel(x, meta_tokens, ln_emb_g, ln_emb_b, w_in, rwkv_mu, rwkv_w0, rwkv_w2, rwkv_a0, rwkv_a2, rwkv_g2, rwkv_k_k, rwkv_k_a, rwkv_r_k, rwkv_gn_g, rwkv_gn_b, fox_b_f, mlstm_conv_w, mlstm_b_i, mlstm_b_f, proj_rwkv, proj_fox, proj_mlstm, w_out, ln1_g, ln1_b, ffn_w1, ffn_w3, ffn_w2, router_w, moe_w1, moe_w3, moe_w2, ln2_g, ln2_b):
    B, S, D = x.shape
    L = N_META + S
    LP = -(-L // CHUNK) * CHUNK
    T = B * LP
    RW = proj_rwkv.shape[1]
    FW = proj_fox.shape[1]
    MW = proj_mlstm.shape[1]
    n_fox_heads = FW // FOX_HEAD_DIM
    dr, ar, gr = RWKV_DECAY_RANK, RWKV_AAA_RANK, RWKV_GATE_RANK
    c1 = 3 * RW + dr + ar + gr
    c2 = c1 + 3 * FW + n_fox_heads
    c3 = c2 + 4 * MW + 2 * MLSTM_HEADS

    meta = jnp.broadcast_to(meta_tokens[None].astype(x.dtype), (B, N_META, D))
    hin = jnp.concatenate([meta, x, jnp.zeros((B, LP - L, D), x.dtype)], axis=1).reshape(T, D)
    h, hb = _ln(hin, ln_emb_g, ln_emb_b)

    for l in range(DEPTH):
        wl = w_in[l]
        w_r = jnp.concatenate([
            _pad_cols(wl[:, 3 * RW:3 * RW + dr], LANES),
            _pad_cols(wl[:, 3 * RW + dr:3 * RW + dr + ar], LANES),
            wl[:, 3 * RW + dr + ar:c1], wl[:, :3 * RW]], axis=1).astype(BF16)
        w_f = wl[:, c1:c1 + 3 * FW].astype(BF16)
        w_m = wl[:, c2:c2 + 4 * MW].astype(BF16)
        w_s = _pad_cols(jnp.concatenate([wl[:, c1 + 3 * FW:c2], wl[:, c2 + 4 * MW:c3]], axis=1),
                        LANES).astype(BF16)
        w_g = wl[:, c3:].astype(BF16)

        z_r = _mm(hb, w_r, BF16, 1408, "in_rwkv").reshape(B, LP, -1)
        z_f = _mm(hb, w_f, BF16, 1152, "in_fox").reshape(B, LP, -1)
        z_m = _mm(hb, w_m, BF16, 1024, "in_mlstm").reshape(B, LP, -1)
        z_s = _mm(hb, w_s, F32, LANES, "in_gates").reshape(B, LP, LANES)

        mu = rwkv_mu[l]
        vecs = jnp.zeros((16, RW), F32)
        for i, p in enumerate([rwkv_w0[l], rwkv_a0[l], rwkv_k_k[l], rwkv_k_a[l],
                               rwkv_r_k[l].reshape(RW), rwkv_gn_g[l], rwkv_gn_b[l]]):
            vecs = vecs.at[i].set(p)
        vecs = vecs.at[8].set(mu[:RW]).at[9].set(mu[RW:2 * RW]).at[10].set(mu[2 * RW:3 * RW])
        mul = jnp.zeros((8, RWKV_LORA_W), F32)
        mul = mul.at[0, 0:dr].set(mu[3 * RW:3 * RW + dr])
        mul = mul.at[0, LANES:LANES + ar].set(mu[3 * RW + dr:3 * RW + dr + ar])
        mul = mul.at[0, 2 * LANES:2 * LANES + gr].set(mu[3 * RW + dr + ar:c1])
        y_r = _rwkv(z_r, vecs, mul, _pad_rows(rwkv_w2[l], LANES).astype(BF16),
                    _pad_rows(rwkv_a2[l], LANES).astype(BF16), rwkv_g2[l].astype(BF16), B, LP)

        bias = jnp.zeros((1, LANES), F32)
        bias = bias.at[0, 0:n_fox_heads].set(fox_b_f[l])
        bias = bias.at[0, MLSTM_GATE_LANE_I:MLSTM_GATE_LANE_I + MLSTM_HEADS].set(mlstm_b_i[l])
        bias = bias.at[0, MLSTM_GATE_LANE_F:MLSTM_GATE_LANE_F + MLSTM_HEADS].set(mlstm_b_f[l])

        ct = _fox_cum(z_s, bias, B, LP, FW // LANES)
        y_f = _fox(z_f, ct, B, LP, FW)

        cw = _pad_rows(mlstm_conv_w[l], SUBLANES)
        y_m = _mlstm(z_m, z_s, bias, cw[:, :MW], cw[:, MW:], B, LP)

        pre = _merge(hb, w_g, y_r.reshape(T, RW), y_f.reshape(T, FW), y_m.reshape(T, MW),
                     proj_rwkv[l].astype(BF16), proj_fox[l].astype(BF16),
                     proj_mlstm[l].astype(BF16))
        h, hb = _mm_res_ln(pre, w_out[l].astype(BF16), h, ln1_g[l], ln1_b[l])

        i = l // 2
        if l % 2 == 0:
            h, hb = _ffn(hb, ffn_w1[i].astype(BF16), ffn_w3[i].astype(BF16),
                         ffn_w2[i].astype(BF16), h, ln2_g[l], ln2_b[l])
        else:
            tm = _tile(T, MOE_TILE)
            n_tiles = 2 * T // tm + N_EXPERTS
            gw, ids = _router(h, router_w[i])
            row_token, dest, tile_expert, n_used = _route(ids, tm, n_tiles)
            xs = _gather_rows(h, row_token)
            ys = _moe_ffn(xs, moe_w1[i].astype(BF16), moe_w3[i].astype(BF16),
                          moe_w2[i].astype(BF16), tile_expert, n_used, tm)
            h, hb = _combine(ys, dest, gw, h, ln2_g[l], ln2_b[l])

    return h.reshape(B, LP, D)[:, N_META:L]
```

```python
import functools
import math

import jax
import jax.numpy as jnp
from jax import lax
from jax.experimental import pallas as pl
from jax.experimental.pallas import tpu as pltpu

F32 = jnp.float32
BF16 = jnp.bfloat16
HI = lax.Precision.HIGHEST

N_META = 16
DEPTH = 4
RWKV_HEAD_DIM = 64
RWKV_DECAY_RANK = 96
RWKV_AAA_RANK = 96
RWKV_GATE_RANK = 256
RWKV_GN_EPS = 64e-5
FOX_HEAD_DIM = 64
MLSTM_HEADS = 4
MLSTM_CONV = 4
N_EXPERTS = 8
LN_EPS = 1e-5
DEEPNORM_ALPHA = (2 * DEPTH) ** 0.25

CHUNK = 64
LANES = 128
SUBLANES = 8
ATT_BLOCK = 256
MOE_TILE = 512
VMEM_LIMIT = 56 * 1024 * 1024
NEG_BIG = -1e30


def _cparams(sem):
    return pltpu.CompilerParams(dimension_semantics=sem, vmem_limit_bytes=VMEM_LIMIT)


def _tile(n, pref, mult=SUBLANES):
    best = None
    for d in range(mult, min(n, pref) + 1, mult):
        if n % d == 0:
            best = d
    return best if best is not None else n


def _dot(a, b, precision=None):
    return jnp.dot(a, b, preferred_element_type=F32, precision=precision)


def _dot_nt(a, b, precision=None):
    return lax.dot_general(a, b, (((1,), (1,)), ((), ())), preferred_element_type=F32,
                           precision=precision)


def _dot_tn(a, b, precision=None):
    return lax.dot_general(a, b, (((0,), (0,)), ((), ())), preferred_element_type=F32,
                           precision=precision)


def _softplus(x):
    return jnp.maximum(x, 0.0) + jnp.log(1.0 + jnp.exp(-jnp.abs(x)))


def _log_sigmoid(x):
    return jnp.minimum(x, 0.0) - jnp.log(1.0 + jnp.exp(-jnp.abs(x)))


def _sigmoid(x):
    return 1.0 / (1.0 + jnp.exp(-x))


def _iota(shape, dim):
    return lax.broadcasted_iota(jnp.int32, shape, dim)


def _layer_norm(t, g, b):
    mu = jnp.mean(t, axis=-1, keepdims=True)
    d = t - mu
    var = jnp.mean(d * d, axis=-1, keepdims=True)
    return d * lax.rsqrt(var + LN_EPS) * g + b


def _ln_kernel(x_ref, g_ref, b_ref, o_ref, ob_ref):
    y = _layer_norm(x_ref[...], g_ref[...], b_ref[...])
    o_ref[...] = y
    ob_ref[...] = y.astype(BF16)


def _ln(x, g, b):
    T, D = x.shape
    bm = _tile(T, 512)
    return pl.pallas_call(
        _ln_kernel,
        grid=(T // bm,),
        in_specs=[pl.BlockSpec((bm, D), lambda i: (i, 0)),
                  pl.BlockSpec((1, D), lambda i: (0, 0)),
                  pl.BlockSpec((1, D), lambda i: (0, 0))],
        out_specs=[pl.BlockSpec((bm, D), lambda i: (i, 0)),
                   pl.BlockSpec((bm, D), lambda i: (i, 0))],
        out_shape=[jax.ShapeDtypeStruct((T, D), F32), jax.ShapeDtypeStruct((T, D), BF16)],
        compiler_params=_cparams(("parallel",)),
        name="ln_embed",
    )(x, g.reshape(1, D), b.reshape(1, D))


def _mm_kernel(a_ref, b_ref, o_ref):
    o_ref[...] = _dot(a_ref[...], b_ref[...]).astype(o_ref.dtype)


def _mm(a, b, out_dtype, bn_pref, name):
    M, K = a.shape
    N = b.shape[1]
    bm = _tile(M, 1024)
    bn = _tile(N, bn_pref, LANES)
    return pl.pallas_call(
        _mm_kernel,
        grid=(M // bm, N // bn),
        in_specs=[pl.BlockSpec((bm, K), lambda i, j: (i, 0)),
                  pl.BlockSpec((K, bn), lambda i, j: (0, j))],
        out_specs=pl.BlockSpec((bm, bn), lambda i, j: (i, j)),
        out_shape=jax.ShapeDtypeStruct((M, N), out_dtype),
        compiler_params=_cparams(("parallel", "arbitrary")),
        name=name,
    )(a, b)


def _mm_res_ln_kernel(a_ref, w_ref, h_ref, g_ref, b_ref, o_ref, ob_ref):
    t = DEEPNORM_ALPHA * h_ref[...] + _dot(a_ref[...], w_ref[...])
    y = _layer_norm(t, g_ref[...], b_ref[...])
    o_ref[...] = y
    ob_ref[...] = y.astype(BF16)


def _mm_res_ln(a, w, h, g, b):
    M, K = a.shape
    D = w.shape[1]
    bm = _tile(M, 512)
    return pl.pallas_call(
        _mm_res_ln_kernel,
        grid=(M // bm,),
        in_specs=[pl.BlockSpec((bm, K), lambda i: (i, 0)),
                  pl.BlockSpec((K, D), lambda i: (0, 0)),
                  pl.BlockSpec((bm, D), lambda i: (i, 0)),
                  pl.BlockSpec((1, D), lambda i: (0, 0)),
                  pl.BlockSpec((1, D), lambda i: (0, 0))],
        out_specs=[pl.BlockSpec((bm, D), lambda i: (i, 0)),
                   pl.BlockSpec((bm, D), lambda i: (i, 0))],
        out_shape=[jax.ShapeDtypeStruct((M, D), F32), jax.ShapeDtypeStruct((M, D), BF16)],
        compiler_params=_cparams(("parallel",)),
        name="out_proj_ln",
    )(a, w, h, g.reshape(1, D), b.reshape(1, D))


def _merge_kernel(hb_ref, wgr_ref, wgf_ref, wgm_ref, yr_ref, yf_ref, ym_ref, pr_ref, pf_ref,
                  pm_ref, o_ref):
    hb = hb_ref[...]
    acc = _sigmoid(_dot(hb, wgr_ref[...])) * _dot(yr_ref[...], pr_ref[...])
    acc += _sigmoid(_dot(hb, wgf_ref[...])) * _dot(yf_ref[...], pf_ref[...])
    acc += _sigmoid(_dot(hb, wgm_ref[...])) * _dot(ym_ref[...], pm_ref[...])
    o_ref[...] = acc.astype(o_ref.dtype)


def _merge(hb, wg, yr, yf, ym, pr, pf, pm):
    M, D = hb.shape
    bm = _tile(M, 1024)
    bn = _tile(D, 512, LANES)
    nb = D // bn
    row = lambda w: pl.BlockSpec((bm, w), lambda i, j: (i, 0))
    col = lambda k: pl.BlockSpec((k, bn), lambda i, j: (0, j))
    gate = lambda br: pl.BlockSpec((D, bn), lambda i, j: (0, br * nb + j))
    return pl.pallas_call(
        _merge_kernel,
        grid=(M // bm, nb),
        in_specs=[row(D), gate(0), gate(1), gate(2),
                  row(yr.shape[1]), row(yf.shape[1]), row(ym.shape[1]),
                  col(pr.shape[0]), col(pf.shape[0]), col(pm.shape[0])],
        out_specs=pl.BlockSpec((bm, bn), lambda i, j: (i, j)),
        out_shape=jax.ShapeDtypeStruct((M, D), BF16),
        compiler_params=_cparams(("parallel", "arbitrary")),
        name="merge",
    )(hb, wg, wg, wg, yr, yf, ym, pr, pf, pm)


def _router_kernel(h_ref, w_ref, ow_ref, oi_ref):
    logits = _dot(h_ref[...], w_ref[...], HI)
    lane = _iota(logits.shape, 1)
    logits = jnp.where(lane < N_EXPERTS, logits, -jnp.inf)
    m1 = jnp.max(logits, axis=-1, keepdims=True)
    i1 = jnp.min(jnp.where(logits == m1, lane, LANES), axis=-1, keepdims=True)
    rest = jnp.where(lane == i1, -jnp.inf, logits)
    m2 = jnp.max(rest, axis=-1, keepdims=True)
    i2 = jnp.min(jnp.where(rest == m2, lane, LANES), axis=-1, keepdims=True)
    e2 = jnp.exp(m2 - m1)
    den = 1.0 + e2
    ow_ref[...] = jnp.where(lane == 0, 1.0 / den, jnp.where(lane == 1, e2 / den, 0.0))
    oi_ref[...] = jnp.where(lane == 0, i1, jnp.where(lane == 1, i2, 0))


def _router(h, w):
    T, D = h.shape
    bm = _tile(T, 512)
    wp = jnp.zeros((D, LANES), F32).at[:, :N_EXPERTS].set(w)
    out = pl.BlockSpec((bm, LANES), lambda i: (i, 0))
    return pl.pallas_call(
        _router_kernel,
        grid=(T // bm,),
        in_specs=[pl.BlockSpec((bm, D), lambda i: (i, 0)),
                  pl.BlockSpec((D, LANES), lambda i: (0, 0))],
        out_specs=[out, out],
        out_shape=[jax.ShapeDtypeStruct((T, LANES), F32),
                   jax.ShapeDtypeStruct((T, LANES), jnp.int32)],
        compiler_params=_cparams(("parallel",)),
        name="router",
    )(h, wp)


def _route(ids, tm, n_tiles):
    T = ids.shape[0]
    e = ids[:, :2].reshape(-1)
    onehot = (e[:, None] == jnp.arange(N_EXPERTS, dtype=jnp.int32)[None, :]).astype(jnp.int32)
    pos = jnp.take_along_axis(jnp.cumsum(onehot, axis=0), e[:, None], axis=1)[:, 0] - 1
    counts = jnp.sum(onehot, axis=0)
    padded = (counts + tm - 1) // tm * tm
    ends = jnp.cumsum(padded)
    dest = (ends - padded)[e] + pos
    row_token = jnp.zeros((n_tiles * tm,), jnp.int32).at[dest].set(
        jnp.arange(2 * T, dtype=jnp.int32) // 2)
    tile_start = jnp.arange(n_tiles, dtype=jnp.int32) * tm
    tile_expert = jnp.minimum(
        jnp.sum((ends[None, :] <= tile_start[:, None]).astype(jnp.int32), axis=1), N_EXPERTS - 1)
    n_used = (ends[-1] // tm).astype(jnp.int32).reshape(1)
    return row_token.reshape(n_tiles, 1, tm), dest, tile_expert, n_used


def _row_copy(src_ref, buf_ref, sem, src_row, dst_row):
    return pltpu.make_async_copy(src_ref.at[pl.ds(src_row, 1), :],
                                 buf_ref.at[pl.ds(dst_row, 1), :], sem.at[0])


def _gather_kernel(idx_ref, src_ref, o_ref, buf_ref, sem):
    tm = buf_ref.shape[0]

    def issue(r, c):
        _row_copy(src_ref, buf_ref, sem, idx_ref[0, 0, r], r).start()
        return c

    def drain(r, c):
        _row_copy(src_ref, buf_ref, sem, 0, r).wait()
        return c

    lax.fori_loop(0, tm, issue, 0, unroll=8)
    lax.fori_loop(0, tm, drain, 0, unroll=8)
    o_ref[...] = buf_ref[...].astype(o_ref.dtype)


def _gather_rows(src, row_token):
    n_tiles, _, tm = row_token.shape
    D = src.shape[1]
    return pl.pallas_call(
        _gather_kernel,
        grid=(n_tiles,),
        in_specs=[pl.BlockSpec((1, 1, tm), lambda i: (i, 0, 0), memory_space=pltpu.SMEM),
                  pl.BlockSpec(memory_space=pl.ANY)],
        out_specs=pl.BlockSpec((tm, D), lambda i: (i, 0)),
        out_shape=jax.ShapeDtypeStruct((n_tiles * tm, D), BF16),
        scratch_shapes=[pltpu.VMEM((tm, D), F32), pltpu.SemaphoreType.DMA((1,))],
        compiler_params=_cparams(("arbitrary",)),
        name="moe_gather",
    )(row_token, src)


def _moe_ffn_kernel(te_ref, nu_ref, x_ref, w1_ref, w3_ref, w2_ref, o_ref):
    j = pl.program_id(1)

    @pl.when(j == 0)
    def _():
        o_ref[...] = jnp.zeros_like(o_ref)

    @pl.when(pl.program_id(0) < nu_ref[0])
    def _():
        _swiglu_accumulate(x_ref, w1_ref[0], w3_ref[0], w2_ref[0], o_ref)


def _moe_ffn(xs, w1, w3, w2, tile_expert, n_used, tm):
    NP, D = xs.shape
    FF = w1.shape[2]
    bf = _tile(FF, 512, LANES)
    grid_spec = pltpu.PrefetchScalarGridSpec(
        num_scalar_prefetch=2,
        grid=(NP // tm, FF // bf),
        in_specs=[pl.BlockSpec((tm, D), lambda i, j, te, nu: (i, 0)),
                  pl.BlockSpec((1, D, bf), lambda i, j, te, nu: (te[i], 0, j)),
                  pl.BlockSpec((1, D, bf), lambda i, j, te, nu: (te[i], 0, j)),
                  pl.BlockSpec((1, bf, D), lambda i, j, te, nu: (te[i], j, 0))],
        out_specs=pl.BlockSpec((tm, D), lambda i, j, te, nu: (i, 0)),
    )
    return pl.pallas_call(
        _moe_ffn_kernel,
        grid_spec=grid_spec,
        out_shape=jax.ShapeDtypeStruct((NP, D), F32),
        compiler_params=_cparams(("parallel", "arbitrary")),
        name="moe_ffn",
    )(tile_expert, n_used, xs, w1, w3, w2)


def _combine_kernel(d_ref, y_ref, gw_ref, h_ref, g_ref, b_ref, o_ref, ob_ref, buf_ref, sem):
    bm = buf_ref.shape[1]

    def issue(t, c):
        for r in range(2):
            _row_copy(y_ref, buf_ref.at[r], sem, d_ref[0, 0, 2 * t + r], t).start()
        return c

    def drain(t, c):
        for r in range(2):
            _row_copy(y_ref, buf_ref.at[r], sem, 0, t).wait()
        return c

    lax.fori_loop(0, bm, issue, 0, unroll=4)
    lax.fori_loop(0, bm, drain, 0, unroll=4)
    gw = gw_ref[...]
    ff = gw[:, 0:1] * buf_ref[0] + gw[:, 1:2] * buf_ref[1]
    y = _layer_norm(DEEPNORM_ALPHA * h_ref[...] + ff, g_ref[...], b_ref[...])
    o_ref[...] = y
    ob_ref[...] = y.astype(BF16)


def _combine(ys, dest, gw, h, g, b):
    T, D = h.shape
    bm = _tile(T, 256)
    row = lambda w: pl.BlockSpec((bm, w), lambda i: (i, 0))
    vec = pl.BlockSpec((1, D), lambda i: (0, 0))
    return pl.pallas_call(
        _combine_kernel,
        grid=(T // bm,),
        in_specs=[pl.BlockSpec((1, 1, 2 * bm), lambda i: (i, 0, 0), memory_space=pltpu.SMEM),
                  pl.BlockSpec(memory_space=pl.ANY), row(LANES), row(D), vec, vec],
        out_specs=[row(D), row(D)],
        out_shape=[jax.ShapeDtypeStruct((T, D), F32), jax.ShapeDtypeStruct((T, D), BF16)],
        scratch_shapes=[pltpu.VMEM((2, bm, D), F32), pltpu.SemaphoreType.DMA((1,))],
        compiler_params=_cparams(("arbitrary",)),
        name="moe_combine",
    )(dest.reshape(T // bm, 1, 2 * bm), ys, gw, h, g.reshape(1, D), b.reshape(1, D))


def _swiglu_accumulate(x_ref, w1, w3, w2, acc_ref):
    hm = x_ref.shape[0] // 2
    rows = [slice(0, hm), slice(hm, 2 * hm)]
    ac = [(_dot(x_ref[r, :], w1), _dot(x_ref[r, :], w3)) for r in rows]
    hh = [(a * _sigmoid(a) * c).astype(BF16) for a, c in ac]
    for r, t in zip(rows, hh):
        acc_ref[r, :] += _dot(t, w2)


def _ffn_kernel(x_ref, w1_ref, w3_ref, w2_ref, h_ref, g_ref, b_ref, o_ref, ob_ref, acc_ref):
    j = pl.program_id(1)

    @pl.when(j == 0)
    def _():
        acc_ref[...] = jnp.zeros_like(acc_ref)

    _swiglu_accumulate(x_ref, w1_ref[...], w3_ref[...], w2_ref[...], acc_ref)

    @pl.when(j == pl.num_programs(1) - 1)
    def _():
        y = _layer_norm(DEEPNORM_ALPHA * h_ref[...] + acc_ref[...], g_ref[...], b_ref[...])
        o_ref[...] = y
        ob_ref[...] = y.astype(BF16)


def _ffn(xb, w1, w3, w2, h, g, b):
    M, D = xb.shape
    FF = w1.shape[1]
    bm = _tile(M, 512)
    bf = _tile(FF, 512, LANES)
    row = lambda w: pl.BlockSpec((bm, w), lambda i, j: (i, 0))
    vec = pl.BlockSpec((1, D), lambda i, j: (0, 0))
    return pl.pallas_call(
        _ffn_kernel,
        grid=(M // bm, FF // bf),
        in_specs=[row(D),
                  pl.BlockSpec((D, bf), lambda i, j: (0, j)),
                  pl.BlockSpec((D, bf), lambda i, j: (0, j)),
                  pl.BlockSpec((bf, D), lambda i, j: (j, 0)),
                  row(D), vec, vec],
        out_specs=[row(D), row(D)],
        out_shape=[jax.ShapeDtypeStruct((M, D), F32), jax.ShapeDtypeStruct((M, D), BF16)],
        scratch_shapes=[pltpu.VMEM((bm, D), F32)],
        compiler_params=_cparams(("parallel", "arbitrary")),
        name="ffn",
    )(xb, w1, w3, w2, h, g.reshape(1, D), b.reshape(1, D))


RWKV_LORA_W = 512
RWKV_SUM_W = 256
BF16_ROWS = 16


def _shift_lerp(ref, c0, width, r0, p0, notfirst, mu, row0):
    cur = ref[0, pl.ds(r0, CHUNK), c0:c0 + width].astype(F32)
    prev = ref[0, pl.ds(p0, BF16_ROWS), c0:c0 + width][BF16_ROWS - 1:BF16_ROWS, :].astype(F32)
    zprev = jnp.where(row0, prev * notfirst, pltpu.roll(cur, 1, 0))
    return cur + mu * (zprev - cur)


def _split3(x):
    hi = x.astype(BF16)
    r1 = x - hi.astype(F32)
    mid = r1.astype(BF16)
    lo = (r1 - mid.astype(F32)).astype(BF16)
    return hi, mid, lo


def _rwkv_kernel(z_ref, vec_ref, mul_ref, w2_ref, a2_ref, g2_ref, o_ref, s_ref):
    C, N = CHUNK, RWKV_HEAD_DIM
    W = o_ref.shape[2]
    H = W // N
    LW = RWKV_LORA_W
    nchunk = z_ref.shape[1] // C
    s_ref[...] = jnp.zeros_like(s_ref)

    ri = _iota((C, C), 0)
    ci = _iota((C, C), 1)
    strict = ri > ci
    incl = ri >= ci
    tri = incl.astype(BF16)
    eye = (ri == ci).astype(F32)
    SW = RWKV_SUM_W
    bd = (_iota((SW, SW), 0) // N == _iota((SW, SW), 1) // N).astype(BF16)
    row0_w = _iota((C, W), 0) == 0
    row0_l = _iota((C, LW), 0) == 0

    def head_sum(x):
        xb = x.astype(BF16)
        return jnp.concatenate([_dot(xb[:, g:g + SW], bd) for g in range(0, W, SW)], axis=1)

    w0 = vec_ref[0:1, :]
    a0 = vec_ref[1:2, :]
    k_k = vec_ref[2:3, :]
    k_a = vec_ref[3:4, :]
    r_k = vec_ref[4:5, :]
    gn_g = vec_ref[5:6, :]
    gn_b = vec_ref[6:7, :]
    mu_r = vec_ref[8:9, :]
    mu_k = vec_ref[9:10, :]
    mu_v = vec_ref[10:11, :]
    mu_l = mul_ref[0:1, :]

    def body(c, carry):
        r0 = pl.multiple_of(c * C, C)
        p0 = pl.multiple_of(jnp.maximum(r0 - BF16_ROWS, 0), BF16_ROWS)
        notfirst = jnp.where(c > 0, 1.0, 0.0).astype(F32)
        lo = _shift_lerp(z_ref, 0, LW, r0, p0, notfirst, mu_l, row0_l)
        r = _shift_lerp(z_ref, LW, W, r0, p0, notfirst, mu_r, row0_w)
        k = _shift_lerp(z_ref, LW + W, W, r0, p0, notfirst, mu_k, row0_w)
        v = _shift_lerp(z_ref, LW + 2 * W, W, r0, p0, notfirst, mu_v, row0_w)

        wd = lo[:, 0:128]
        ad = lo[:, 128:256]
        gd = lo[:, 256:512]
        w_log = -_softplus(-(w0 + _dot(jnp.tanh(wd).astype(BF16), w2_ref[...]))) - 0.5
        lw = -jnp.exp(w_log)
        a = _sigmoid(a0 + _dot(ad.astype(BF16), a2_ref[...]))
        g = _dot(_sigmoid(gd).astype(BF16), g2_ref[...])

        kk = k * k_k
        kk = kk / jnp.maximum(jnp.sqrt(head_sum(kk * kk)), 1e-12)
        kmod = k * (1.0 + (a - 1.0) * k_a)
        b = kk * a

        cum = sum(_dot(tri, part) for part in _split3(lw))
        cum_c = cum[C - 1:C, :]
        e_inv = jnp.exp(-cum)
        e_fin = jnp.exp(cum_c - cum)
        kkt = kk * jnp.exp(cum - lw)
        rt = r * jnp.exp(cum)
        bt = b * e_inv
        kt = kmod * e_inv
        bh = b * e_fin
        kh = kmod * e_fin
        pc = jnp.exp(cum_c)

        hs = range(H)
        sl = [slice(h * N, (h + 1) * N) for h in hs]
        cat = lambda top, bot, h: jnp.concatenate([top[:, sl[h]], bot[:, sl[h]]], axis=0).astype(BF16)
        lhs = [cat(kkt, rt, h) for h in hs]
        rhs = [cat(bt, kt, h) for h in hs]
        vb = [v[:, sl[h]].astype(BF16) for h in hs]
        s_old = [s_ref[h] for h in hs]
        mall = [_dot_nt(lhs[h], rhs[h]) for h in hs]
        gm = [_dot_nt(lhs[h], s_old[h].astype(BF16)) for h in hs]
        mb = [jnp.where(incl, mall[h][C:, :C], 0.0).astype(BF16) for h in hs]
        akmk = [jnp.concatenate([jnp.where(strict, mall[h][:C, C:], 0.0),
                                 jnp.where(incl, mall[h][C:, C:], 0.0)], axis=0).astype(BF16)
                for h in hs]
        av = [_dot(akmk[h], vb[h]) for h in hs]
        x = [jnp.where(strict, -mall[h][:C, :C], 0.0) for h in hs]
        t = [eye + x[h] for h in hs]
        for _ in range(5):
            xb = [x[h].astype(BF16) for h in hs]
            x = [_dot(xb[h], xb[h]) for h in hs]
            t = [t[h] + _dot(t[h].astype(BF16), x[h].astype(BF16)) for h in hs]
        u = [_dot(t[h].astype(BF16), (gm[h][:C] + av[h][:C]).astype(BF16)) for h in hs]
        o = [gm[h][C:] + av[h][C:] - _dot(mb[h], u[h].astype(BF16)) for h in hs]
        upd = [_dot_tn(jnp.concatenate([u[h].astype(BF16), vb[h]], axis=0), cat(-bh, kh, h))
               for h in hs]
        for h in hs:
            s_ref[h] = s_old[h] * pc[:, sl[h]] + upd[h]

        o = jnp.concatenate(o, axis=1)
        d = o - head_sum(o) * (1.0 / N)
        var = head_sum(d * d) * (1.0 / N)
        on = d * lax.rsqrt(var + RWKV_GN_EPS) * gn_g + gn_b
        bonus = head_sum(r * kmod * r_k) * v
        o_ref[0, pl.ds(r0, C), :] = ((on + bonus) * g).astype(o_ref.dtype)
        return carry

    lax.fori_loop(0, nchunk, body, 0)


def _rwkv(z, vecs, mul, w2, a2, g2, B, LP):
    W = vecs.shape[1]
    ZW = z.shape[2]
    whole = lambda shape: pl.BlockSpec(shape, lambda b: (0,) * len(shape))
    return pl.pallas_call(
        _rwkv_kernel,
        grid=(B,),
        in_specs=[pl.BlockSpec((1, LP, ZW), lambda b: (b, 0, 0)),
                  whole((16, W)), whole((8, RWKV_LORA_W)), whole((LANES, W)), whole((LANES, W)),
                  whole((RWKV_GATE_RANK, W))],
        out_specs=pl.BlockSpec((1, LP, W), lambda b: (b, 0, 0)),
        out_shape=jax.ShapeDtypeStruct((B, LP, W), BF16),
        scratch_shapes=[pltpu.VMEM((W // RWKV_HEAD_DIM, RWKV_HEAD_DIM, RWKV_HEAD_DIM), F32)],
        compiler_params=_cparams(("parallel",)),
        name="rwkv7",
    )(z, vecs, mul, w2, a2, g2)


def _fox_cum_kernel(zs_ref, bias_ref, o_ref):
    LP = zs_ref.shape[1]
    nkb = o_ref.shape[2]
    TB = ATT_BLOCK
    o_ref[...] = jnp.zeros_like(o_ref)
    sel = (_iota((16, LANES), 0) == _iota((16, LANES), 1)).astype(F32)
    carry = jnp.zeros((16, 1), F32)
    for kb in range(nkb):
        n = min(TB, LP - kb * TB)
        lf = _log_sigmoid(zs_ref[0, kb * TB:kb * TB + n, :] + bias_ref[...])
        lft = _dot_nt(sel, lf, HI)
        upper = (_iota((n, n), 0) <= _iota((n, n), 1)).astype(F32)
        cs = _dot(lft, upper, HI) + carry
        carry = cs[:, n - 1:n]
        for hp in range(o_ref.shape[1]):
            o_ref[0, hp, kb, 0:2, 0:n] = cs[2 * hp:2 * hp + 2, :]


def _fox_cum(zs, bias, B, LP, n_pairs):
    nkb = pl.cdiv(LP, ATT_BLOCK)
    return pl.pallas_call(
        _fox_cum_kernel,
        grid=(B,),
        in_specs=[pl.BlockSpec((1, LP, LANES), lambda b: (b, 0, 0)),
                  pl.BlockSpec((1, LANES), lambda b: (0, 0))],
        out_specs=pl.BlockSpec((1, n_pairs, nkb, SUBLANES, ATT_BLOCK), lambda b: (b, 0, 0, 0, 0)),
        out_shape=jax.ShapeDtypeStruct((B, n_pairs, nkb, SUBLANES, ATT_BLOCK), F32),
        compiler_params=_cparams(("parallel",)),
        name="fox_cum",
    )(zs, bias)


def _fox_kernel(q_ref, k_ref, v_ref, ct_ref, o_ref):
    TB, N = ATT_BLOCK, FOX_HEAD_DIM
    LP = k_ref.shape[1]
    n_full = LP // TB
    tail = LP - n_full * TB
    qi = pl.program_id(2)

    def attend(rows):
        halves = 2 if rows == TB else 1
        HQ = rows // halves
        units = [(h, s) for h in range(2) for s in range(halves)]
        nu = len(units)
        lane_q = _iota((HQ, LANES), 1)
        scale = N ** -0.5
        qu = [jnp.where((lane_q < N) == (h == 0), q_ref[0, s * HQ:(s + 1) * HQ, :] * scale, 0.0)
              for h, s in units]

        def block(k0, n, cks, masks, state):
            kblk = k_ref[0, pl.ds(k0, n), :]
            vblk = v_ref[0, pl.ds(k0, n), :]
            lane_v = _iota((n, LANES), 1)
            vaug = [jnp.where((lane_v < N) == (h == 0), vblk, 1.0) for h in range(2)]
            sc = [_dot_nt(qu[i], kblk) - cks[h] for i, (h, s) in enumerate(units)]
            if masks is not None:
                sc = [jnp.where(masks[s], sc[i], -jnp.inf) for i, (h, s) in enumerate(units)]
            m_new = [jnp.maximum(state[i][0], jnp.max(sc[i], axis=-1, keepdims=True))
                     for i in range(nu)]
            p = [jnp.exp(sc[i] - m_new[i]).astype(BF16) for i in range(nu)]
            pv = [_dot(p[i], vaug[h]) for i, (h, s) in enumerate(units)]
            return tuple((m_new[i], jnp.exp(state[i][0] - m_new[i]) * state[i][1] + pv[i])
                         for i in range(nu))

        ck = lambda kb, h: ct_ref[0, 0, kb, h:h + 1, :]

        def pair_body(kp, state):
            cks = [jnp.concatenate([ck(2 * kp, h), ck(2 * kp + 1, h)], axis=1) for h in range(2)]
            return block(pl.multiple_of(kp * 2 * TB, 2 * TB), 2 * TB, cks, None, state)

        def single_body(_, state):
            kb = qi - 1
            return block(pl.multiple_of(kb * TB, TB), TB, [ck(kb, h) for h in range(2)], None,
                         state)

        state = tuple((jnp.full((HQ, 1), NEG_BIG, F32), jnp.zeros((HQ, LANES), F32))
                      for _ in units)
        state = lax.fori_loop(0, lax.shift_right_logical(qi, 1), pair_body, state)
        state = lax.fori_loop(0, qi & 1, single_body, state)
        cks = [ck(qi, h)[:, 0:rows] for h in range(2)]
        masks = [_iota((HQ, rows), 0) + s * HQ >= _iota((HQ, rows), 1) for s in range(halves)]
        final = block(pl.multiple_of(qi * TB, TB), rows, cks, masks, state)
        for s in range(halves):
            acc0, acc1 = final[s][1], final[halves + s][1]
            out = jnp.where(lane_q < N, acc0 / acc0[:, N:N + 1], acc1 / acc1[:, 0:1])
            o_ref[0, s * HQ:(s + 1) * HQ, :] = out.astype(o_ref.dtype)

    if tail == 0:
        attend(TB)
    else:
        pl.when(qi < n_full)(lambda: attend(TB))
        pl.when(qi == n_full)(lambda: attend(tail))


def _fox(zq, ct, B, LP, width):
    n_pairs = width // LANES
    nq = pl.cdiv(LP, ATT_BLOCK)
    nkb = ct.shape[2]
    kv = lambda off: pl.BlockSpec((1, LP, LANES), lambda b, p, i: (b, 0, off + p))
    return pl.pallas_call(
        _fox_kernel,
        grid=(B, n_pairs, nq),
        in_specs=[pl.BlockSpec((1, ATT_BLOCK, LANES), lambda b, p, i: (b, i, p)),
                  kv(n_pairs), kv(2 * n_pairs),
                  pl.BlockSpec((1, 1, nkb, SUBLANES, ATT_BLOCK), lambda b, p, i: (b, p, 0, 0, 0))],
        out_specs=pl.BlockSpec((1, ATT_BLOCK, LANES), lambda b, p, i: (b, i, p)),
        out_shape=jax.ShapeDtypeStruct((B, LP, width), BF16),
        compiler_params=_cparams(("parallel", "parallel", "arbitrary")),
        name="fox_attention",
    )(zq, zq, zq, ct)


MLSTM_GATE_LANE_I = 12
MLSTM_GATE_LANE_F = 16


def _conv_silu(ref, r0, p0, notfirst, w_ref):
    cur = ref[0, pl.ds(r0, CHUNK), :].astype(F32)
    prev = ref[0, pl.ds(p0, BF16_ROWS), :][BF16_ROWS - SUBLANES:, :].astype(F32) * notfirst
    row8 = _iota(prev.shape, 0)
    y = cur * w_ref[MLSTM_CONV - 1:MLSTM_CONV, :]
    for s in range(1, MLSTM_CONV):
        rolled = pltpu.roll(cur, s, 0)
        top = jnp.where(row8 < s, pltpu.roll(prev, s, 0), rolled[0:SUBLANES])
        shifted = jnp.concatenate([top, rolled[SUBLANES:]], axis=0)
        y = y + shifted * w_ref[MLSTM_CONV - 1 - s:MLSTM_CONV - s, :]
    return y * _sigmoid(y)


def _mlstm_kernel(q_ref, k_ref, v_ref, og_ref, zs_ref, bias_ref, cwq_ref, cwk_ref,
                  o_ref, c_ref, m_ref):
    C, H = CHUNK, MLSTM_HEADS
    Dh = q_ref.shape[2] // H
    nchunk = q_ref.shape[1] // C
    c_ref[...] = jnp.zeros_like(c_ref)
    m_ref[...] = jnp.zeros_like(m_ref)

    ri = _iota((C, C), 0)
    ci = _iota((C, C), 1)
    incl = ri >= ci
    tri = incl.astype(F32)
    upper = (ri <= ci).astype(F32)
    sel = (_iota((SUBLANES, LANES), 1) == _iota((SUBLANES, LANES), 0) + MLSTM_GATE_LANE_I).astype(F32)
    ones_col = (_iota((C, LANES), 1) == 0).astype(BF16)
    lane = _iota((C, LANES), 1)

    def body(c, carry):
        r0 = pl.multiple_of(c * C, C)
        p0 = pl.multiple_of(jnp.maximum(r0 - BF16_ROWS, 0), BF16_ROWS)
        notfirst = jnp.where(c > 0, 1.0, 0.0).astype(F32)
        q = _conv_silu(q_ref, r0, p0, notfirst, cwq_ref)
        k = _conv_silu(k_ref, r0, p0, notfirst, cwk_ref) * (Dh ** -0.5)
        v = v_ref[0, pl.ds(r0, C), :]
        og = og_ref[0, pl.ds(r0, C), :].astype(F32)
        gz = zs_ref[0, pl.ds(r0, C), :] + bias_ref[...]
        x = jnp.where(lane < MLSTM_GATE_LANE_F, gz, _log_sigmoid(gz))
        xt = _dot_nt(sel, x, HI)
        xc = _dot_nt(x, sel, HI)
        bt = _dot(xt, upper, HI)
        bc = _dot(tri, xc, HI)

        hs = range(H)
        sl = [slice(h * Dh, (h + 1) * Dh) for h in hs]
        m_all = m_ref[...]
        m_old = [m_all[h:h + 1, 0:1] for h in hs]
        c_old = [c_ref[h] for h in hs]
        li_r = [xt[h:h + 1, :] for h in hs]
        b_r = [bt[H + h:H + h + 1, :] for h in hs]
        li_c = [xc[:, h:h + 1] for h in hs]
        b_c = [bc[:, H + h:H + h + 1] for h in hs]
        qh = [q[:, sl[h]].astype(BF16) for h in hs]
        kh = [k[:, sl[h]] for h in hs]
        vaug = [jnp.concatenate([v[:, sl[h]], ones_col], axis=1) for h in hs]
        qk = [_dot_nt(qh[h], kh[h].astype(BF16)) for h in hs]
        qc = [_dot(qh[h], c_old[h].astype(BF16)) for h in hs]
        log_d = [jnp.where(incl, b_c[h] - b_r[h] + li_r[h], -jnp.inf) for h in hs]
        log_inter = [b_c[h] + m_old[h] for h in hs]
        m_q = [jnp.maximum(log_inter[h], jnp.max(log_d[h], axis=-1, keepdims=True)) for h in hs]
        s = [(qk[h] * jnp.exp(log_d[h] - m_q[h])).astype(BF16) for h in hs]
        b_end = [b_c[h][C - 1:C, :] for h in hs]
        m_new = [jnp.maximum(b_end[h] + m_old[h],
                             jnp.max(b_end[h] - b_r[h] + li_r[h], axis=-1, keepdims=True))
                 for h in hs]
        gk = [jnp.exp(b_end[h] - b_c[h] + li_c[h] - m_new[h]) for h in hs]
        num = [jnp.exp(log_inter[h] - m_q[h]) * qc[h] + _dot(s[h], vaug[h]) for h in hs]
        kv = [_dot_tn((gk[h] * kh[h]).astype(BF16), vaug[h]) for h in hs]
        outs = [_sigmoid(og[:, sl[h]]) * num[h][:, :Dh]
                / jnp.maximum(jnp.abs(num[h][:, Dh:Dh + 1]), jnp.exp(-m_q[h])) for h in hs]
        o_ref[0, pl.ds(r0, C), :] = jnp.concatenate(outs, axis=1).astype(o_ref.dtype)
        for h in hs:
            c_ref[h] = jnp.exp(b_end[h] + m_old[h] - m_new[h]) * c_old[h] + kv[h]
        m_ref[0:H, :] = jnp.concatenate([jnp.broadcast_to(m_new[h], (1, LANES)) for h in hs],
                                        axis=0)
        return carry

    lax.fori_loop(0, nchunk, body, 0)


def _mlstm(zm, zs, bias, cwq, cwk, B, LP):
    W = zm.shape[2] // 4
    Dh = W // MLSTM_HEADS
    seq = lambda j: pl.BlockSpec((1, LP, W), lambda b: (b, 0, j))
    cw = pl.BlockSpec((SUBLANES, W), lambda b: (0, 0))
    return pl.pallas_call(
        _mlstm_kernel,
        grid=(B,),
        in_specs=[seq(0), seq(1), seq(2), seq(3),
                  pl.BlockSpec((1, LP, LANES), lambda b: (b, 0, 0)),
                  pl.BlockSpec((1, LANES), lambda b: (0, 0)), cw, cw],
        out_specs=pl.BlockSpec((1, LP, W), lambda b: (b, 0, 0)),
        out_shape=jax.ShapeDtypeStruct((B, LP, W), BF16),
        scratch_shapes=[pltpu.VMEM((MLSTM_HEADS, Dh, 2 * Dh), F32),
                        pltpu.VMEM((SUBLANES, LANES), F32)],
        compiler_params=_cparams(("parallel",)),
        name="mlstm",
    )(zm, zm, zm, zm, zs, bias, cwq, cwk)


def _pad_cols(w, n):
    return jnp.pad(w, ((0, 0), (0, n - w.shape[1])))


def _pad_rows(w, n):
    return jnp.pad(w, ((0, n - w.shape[0]), (0, 0)))


def kernel(x, meta_tokens, ln_emb_g, ln_emb_b, w_in, rwkv_mu, rwkv_w0, rwkv_w2, rwkv_a0, rwkv_a2, rwkv_g2, rwkv_k_k, rwkv_k_a, rwkv_r_k, rwkv_gn_g, rwkv_gn_b, fox_b_f, mlstm_conv_w, mlstm_b_i, mlstm_b_f, proj_rwkv, proj_fox, proj_mlstm, w_out, ln1_g, ln1_b, ffn_w1, ffn_w3, ffn_w2, router_w, moe_w1, moe_w3, moe_w2, ln2_g, ln2_b):
    B, S, D = x.shape
    L = N_META + S
    LP = -(-L // CHUNK) * CHUNK
    T = B * LP
    RW = proj_rwkv.shape[1]
    FW = proj_fox.shape[1]
    MW = proj_mlstm.shape[1]
    n_fox_heads = FW // FOX_HEAD_DIM
    dr, ar, gr = RWKV_DECAY_RANK, RWKV_AAA_RANK, RWKV_GATE_RANK
    c1 = 3 * RW + dr + ar + gr
    c2 = c1 + 3 * FW + n_fox_heads
    c3 = c2 + 4 * MW + 2 * MLSTM_HEADS

    meta = jnp.broadcast_to(meta_tokens[None].astype(x.dtype), (B, N_META, D))
    hin = jnp.concatenate([meta, x, jnp.zeros((B, LP - L, D), x.dtype)], axis=1).reshape(T, D)
    h, hb = _ln(hin, ln_emb_g, ln_emb_b)

    for l in range(DEPTH):
        wl = lax.optimization_barrier(w_in[l])
        w_r = jnp.concatenate([
            _pad_cols(wl[:, 3 * RW:3 * RW + dr], LANES),
            _pad_cols(wl[:, 3 * RW + dr:3 * RW + dr + ar], LANES),
            wl[:, 3 * RW + dr + ar:c1], wl[:, :3 * RW]], axis=1).astype(BF16)
        w_f = wl[:, c1:c1 + 3 * FW].astype(BF16)
        w_m = wl[:, c2:c2 + 4 * MW].astype(BF16)
        w_s = _pad_cols(jnp.concatenate([wl[:, c1 + 3 * FW:c2], wl[:, c2 + 4 * MW:c3]], axis=1),
                        LANES).astype(BF16)
        w_g = wl[:, c3:].astype(BF16)

        z_r = _mm(hb, w_r, BF16, 1408, "in_rwkv").reshape(B, LP, -1)
        z_f = _mm(hb, w_f, BF16, 1152, "in_fox").reshape(B, LP, -1)
        z_m = _mm(hb, w_m, BF16, 1024, "in_mlstm").reshape(B, LP, -1)
        z_s = _mm(hb, w_s, F32, LANES, "in_gates").reshape(B, LP, LANES)

        mu = rwkv_mu[l]
        vecs = jnp.zeros((16, RW), F32)
        for i, p in enumerate([rwkv_w0[l], rwkv_a0[l], rwkv_k_k[l], rwkv_k_a[l],
                               rwkv_r_k[l].reshape(RW), rwkv_gn_g[l], rwkv_gn_b[l]]):
            vecs = vecs.at[i].set(p)
        vecs = vecs.at[8].set(mu[:RW]).at[9].set(mu[RW:2 * RW]).at[10].set(mu[2 * RW:3 * RW])
        mul = jnp.zeros((8, RWKV_LORA_W), F32)
        mul = mul.at[0, 0:dr].set(mu[3 * RW:3 * RW + dr])
        mul = mul.at[0, LANES:LANES + ar].set(mu[3 * RW + dr:3 * RW + dr + ar])
        mul = mul.at[0, 2 * LANES:2 * LANES + gr].set(mu[3 * RW + dr + ar:c1])
        y_r = _rwkv(z_r, vecs, mul, _pad_rows(rwkv_w2[l], LANES).astype(BF16),
                    _pad_rows(rwkv_a2[l], LANES).astype(BF16), rwkv_g2[l].astype(BF16), B, LP)

        bias = jnp.zeros((1, LANES), F32)
        bias = bias.at[0, 0:n_fox_heads].set(fox_b_f[l])
        bias = bias.at[0, MLSTM_GATE_LANE_I:MLSTM_GATE_LANE_I + MLSTM_HEADS].set(mlstm_b_i[l])
        bias = bias.at[0, MLSTM_GATE_LANE_F:MLSTM_GATE_LANE_F + MLSTM_HEADS].set(mlstm_b_f[l])

        ct = _fox_cum(z_s, bias, B, LP, FW // LANES)
        y_f = _fox(z_f, ct, B, LP, FW)

        cw = _pad_rows(mlstm_conv_w[l], SUBLANES)
        y_m = _mlstm(z_m, z_s, bias, cw[:, :MW], cw[:, MW:], B, LP)

        pre = _merge(hb, w_g, y_r.reshape(T, RW), y_f.reshape(T, FW), y_m.reshape(T, MW),
                     proj_rwkv[l].astype(BF16), proj_fox[l].astype(BF16),
                     proj_mlstm[l].astype(BF16))
        h, hb = _mm_res_ln(pre, w_out[l].astype(BF16), h, ln1_g[l], ln1_b[l])

        i = l // 2
        if l % 2 == 0:
            h, hb = _ffn(hb, ffn_w1[i].astype(BF16), ffn_w3[i].astype(BF16),
                         ffn_w2[i].astype(BF16), h, ln2_g[l], ln2_b[l])
        else:
            tm = _tile(T, MOE_TILE)
            n_tiles = 2 * T // tm + N_EXPERTS
            gw, ids = _router(h, router_w[i])
            row_token, dest, tile_expert, n_used = _route(ids, tm, n_tiles)
            xs = _gather_rows(h, row_token)
            ys = _moe_ffn(xs, moe_w1[i].astype(BF16), moe_w3[i].astype(BF16),
                          moe_w2[i].astype(BF16), tile_expert, n_used, tm)
            h, hb = _combine(ys, dest, gw, h, ln2_g[l], ln2_b[l])

    return h.reshape(B, LP, D)[:, N_META:L]
```

```python
import functools
import math

import jax
import jax.numpy as jnp
from jax import lax
from jax.experimental import pallas as pl
from jax.experimental.pallas import tpu as pltpu

F32 = jnp.float32
BF16 = jnp.bfloat16
HI = lax.Precision.HIGHEST

N_META = 16
DEPTH = 4
RWKV_HEAD_DIM = 64
RWKV_DECAY_RANK = 96
RWKV_AAA_RANK = 96
RWKV_GATE_RANK = 256
RWKV_GN_EPS = 64e-5
FOX_HEAD_DIM = 64
MLSTM_HEADS = 4
MLSTM_CONV = 4
N_EXPERTS = 8
LN_EPS = 1e-5
DEEPNORM_ALPHA = (2 * DEPTH) ** 0.25

CHUNK = 64
LANES = 128
SUBLANES = 8
BF16_ROWS = 16
ATT_BLOCK = 256
MOE_TILE = 512
DMA_UNROLL = 8
VMEM_LIMIT = 56 * 1024 * 1024
NEG_BIG = -1e30


def _cparams(sem):
    return pltpu.CompilerParams(dimension_semantics=sem, vmem_limit_bytes=VMEM_LIMIT)


def _tile(n, pref, mult=SUBLANES):
    best = None
    for d in range(mult, min(n, pref) + 1, mult):
        if n % d == 0:
            best = d
    return best if best is not None else n


def _dot(a, b, precision=None):
    return jnp.dot(a, b, preferred_element_type=F32, precision=precision)


def _dot_nt(a, b, precision=None):
    return lax.dot_general(a, b, (((1,), (1,)), ((), ())), preferred_element_type=F32,
                           precision=precision)


def _dot_tn(a, b, precision=None):
    return lax.dot_general(a, b, (((0,), (0,)), ((), ())), preferred_element_type=F32,
                           precision=precision)


def _softplus(x):
    return jnp.maximum(x, 0.0) + jnp.log(1.0 + jnp.exp(-jnp.abs(x)))


def _log_sigmoid(x):
    return jnp.minimum(x, 0.0) - jnp.log(1.0 + jnp.exp(-jnp.abs(x)))


def _sigmoid(x):
    return 1.0 / (1.0 + jnp.exp(-x))


def _iota(shape, dim):
    return lax.broadcasted_iota(jnp.int32, shape, dim)


def _layer_norm(t, g, b):
    mu = jnp.mean(t, axis=-1, keepdims=True)
    d = t - mu
    var = jnp.mean(d * d, axis=-1, keepdims=True)
    return d * lax.rsqrt(var + LN_EPS) * g + b


def _ln_kernel(x_ref, g_ref, b_ref, o_ref, ob_ref):
    y = _layer_norm(x_ref[...], g_ref[...], b_ref[...])
    o_ref[...] = y
    ob_ref[...] = y.astype(BF16)


def _ln(x, g, b):
    T, D = x.shape
    bm = _tile(T, 512)
    return pl.pallas_call(
        _ln_kernel,
        grid=(T // bm,),
        in_specs=[pl.BlockSpec((bm, D), lambda i: (i, 0)),
                  pl.BlockSpec((1, D), lambda i: (0, 0)),
                  pl.BlockSpec((1, D), lambda i: (0, 0))],
        out_specs=[pl.BlockSpec((bm, D), lambda i: (i, 0)),
                   pl.BlockSpec((bm, D), lambda i: (i, 0))],
        out_shape=[jax.ShapeDtypeStruct((T, D), F32), jax.ShapeDtypeStruct((T, D), BF16)],
        compiler_params=_cparams(("parallel",)),
        name="ln_embed",
    )(x, g.reshape(1, D), b.reshape(1, D))


def _prep_w_in_kernel(w_ref, or_ref, of_ref, om_ref, os_ref, og_ref, *, rw, fw, mw, dr, ar):
    x = w_ref[0]
    c1 = or_ref.shape[1] - 2 * LANES + dr + ar
    c2 = c1 + 3 * fw + fw // FOX_HEAD_DIM
    c3 = c2 + 4 * mw + 2 * MLSTM_HEADS
    zeros = lambda n: jnp.zeros((x.shape[0], n), F32)
    cast = lambda parts: jnp.concatenate(parts, axis=1).astype(BF16)
    or_ref[...] = cast([x[:, 3 * rw:3 * rw + dr], zeros(LANES - dr),
                        x[:, 3 * rw + dr:3 * rw + dr + ar], zeros(LANES - ar),
                        x[:, 3 * rw + dr + ar:c1], x[:, 0:3 * rw]])
    of_ref[...] = x[:, c1:c1 + 3 * fw].astype(BF16)
    om_ref[...] = x[:, c2:c2 + 4 * mw].astype(BF16)
    n_small = c2 - (c1 + 3 * fw) + c3 - (c2 + 4 * mw)
    os_ref[...] = cast([x[:, c1 + 3 * fw:c2], x[:, c2 + 4 * mw:c3], zeros(LANES - n_small)])
    og_ref[...] = x[:, c3:].astype(BF16)


def _prep_w_in(w_in, l, rw, fw, mw, dr, ar, gr):
    _, D, n_in = w_in.shape
    rows = _tile(D, 128, BF16_ROWS)
    widths = [3 * rw + 2 * LANES + gr, 3 * fw, 4 * mw, LANES, 3 * D]
    return pl.pallas_call(
        functools.partial(_prep_w_in_kernel, rw=rw, fw=fw, mw=mw, dr=dr, ar=ar),
        grid=(D // rows,),
        in_specs=[pl.BlockSpec((1, rows, n_in), lambda i: (l, i, 0))],
        out_specs=[pl.BlockSpec((rows, w), lambda i: (i, 0)) for w in widths],
        out_shape=[jax.ShapeDtypeStruct((D, w), BF16) for w in widths],
        compiler_params=_cparams(("parallel",)),
        name="prep_w_in",
    )(w_in)


def _mm_kernel(a_ref, b_ref, o_ref):
    o_ref[...] = _dot(a_ref[...], b_ref[...]).astype(o_ref.dtype)


def _mm(a, b, out_dtype, bn_pref, name):
    M, K = a.shape
    N = b.shape[1]
    bm = _tile(M, 1024)
    bn = _tile(N, bn_pref, LANES)
    return pl.pallas_call(
        _mm_kernel,
        grid=(M // bm, N // bn),
        in_specs=[pl.BlockSpec((bm, K), lambda i, j: (i, 0)),
                  pl.BlockSpec((K, bn), lambda i, j: (0, j))],
        out_specs=pl.BlockSpec((bm, bn), lambda i, j: (i, j)),
        out_shape=jax.ShapeDtypeStruct((M, N), out_dtype),
        compiler_params=_cparams(("parallel", "arbitrary")),
        name=name,
    )(a, b)


def _mm_res_ln_kernel(a_ref, w_ref, h_ref, g_ref, b_ref, o_ref, ob_ref):
    t = DEEPNORM_ALPHA * h_ref[...] + _dot(a_ref[...], w_ref[...])
    y = _layer_norm(t, g_ref[...], b_ref[...])
    o_ref[...] = y
    ob_ref[...] = y.astype(BF16)


def _mm_res_ln(a, w, h, g, b):
    M, K = a.shape
    D = w.shape[1]
    bm = _tile(M, 512)
    return pl.pallas_call(
        _mm_res_ln_kernel,
        grid=(M // bm,),
        in_specs=[pl.BlockSpec((bm, K), lambda i: (i, 0)),
                  pl.BlockSpec((K, D), lambda i: (0, 0)),
                  pl.BlockSpec((bm, D), lambda i: (i, 0)),
                  pl.BlockSpec((1, D), lambda i: (0, 0)),
                  pl.BlockSpec((1, D), lambda i: (0, 0))],
        out_specs=[pl.BlockSpec((bm, D), lambda i: (i, 0)),
                   pl.BlockSpec((bm, D), lambda i: (i, 0))],
        out_shape=[jax.ShapeDtypeStruct((M, D), F32), jax.ShapeDtypeStruct((M, D), BF16)],
        compiler_params=_cparams(("parallel",)),
        name="out_proj_ln",
    )(a, w, h, g.reshape(1, D), b.reshape(1, D))


def _merge_kernel(hb_ref, wgr_ref, wgf_ref, wgm_ref, yr_ref, yf_ref, ym_ref, pr_ref, pf_ref,
                  pm_ref, o_ref):
    hb = hb_ref[...]
    acc = _sigmoid(_dot(hb, wgr_ref[...])) * _dot(yr_ref[...], pr_ref[...])
    acc += _sigmoid(_dot(hb, wgf_ref[...])) * _dot(yf_ref[...], pf_ref[...])
    acc += _sigmoid(_dot(hb, wgm_ref[...])) * _dot(ym_ref[...], pm_ref[...])
    o_ref[...] = acc.astype(o_ref.dtype)


def _merge(hb, wg, yr, yf, ym, pr, pf, pm):
    M, D = hb.shape
    bm = _tile(M, 1024)
    bn = _tile(D, 512, LANES)
    nb = D // bn
    row = lambda w: pl.BlockSpec((bm, w), lambda i, j: (i, 0))
    col = lambda k: pl.BlockSpec((k, bn), lambda i, j: (0, j))
    gate = lambda br: pl.BlockSpec((D, bn), lambda i, j: (0, br * nb + j))
    return pl.pallas_call(
        _merge_kernel,
        grid=(M // bm, nb),
        in_specs=[row(D), gate(0), gate(1), gate(2),
                  row(yr.shape[1]), row(yf.shape[1]), row(ym.shape[1]),
                  col(pr.shape[0]), col(pf.shape[0]), col(pm.shape[0])],
        out_specs=pl.BlockSpec((bm, bn), lambda i, j: (i, j)),
        out_shape=jax.ShapeDtypeStruct((M, D), BF16),
        compiler_params=_cparams(("parallel", "arbitrary")),
        name="merge",
    )(hb, wg, wg, wg, yr, yf, ym, pr, pf, pm)


def _router_kernel(h_ref, w_ref, ow_ref, oi_ref):
    logits = _dot(h_ref[...], w_ref[...], HI)
    lane = _iota(logits.shape, 1)
    logits = jnp.where(lane < N_EXPERTS, logits, -jnp.inf)
    m1 = jnp.max(logits, axis=-1, keepdims=True)
    i1 = jnp.min(jnp.where(logits == m1, lane, LANES), axis=-1, keepdims=True)
    rest = jnp.where(lane == i1, -jnp.inf, logits)
    m2 = jnp.max(rest, axis=-1, keepdims=True)
    i2 = jnp.min(jnp.where(rest == m2, lane, LANES), axis=-1, keepdims=True)
    e2 = jnp.exp(m2 - m1)
    den = 1.0 + e2
    ow_ref[...] = jnp.where(lane == 0, 1.0 / den, jnp.where(lane == 1, e2 / den, 0.0))
    oi_ref[...] = jnp.where(lane == 0, i1, jnp.where(lane == 1, i2, 0))


def _router(h, w):
    T, D = h.shape
    bm = _tile(T, 512)
    wp = jnp.zeros((D, LANES), F32).at[:, :N_EXPERTS].set(w)
    out = pl.BlockSpec((bm, LANES), lambda i: (i, 0))
    return pl.pallas_call(
        _router_kernel,
        grid=(T // bm,),
        in_specs=[pl.BlockSpec((bm, D), lambda i: (i, 0)),
                  pl.BlockSpec((D, LANES), lambda i: (0, 0))],
        out_specs=[out, out],
        out_shape=[jax.ShapeDtypeStruct((T, LANES), F32),
                   jax.ShapeDtypeStruct((T, LANES), jnp.int32)],
        compiler_params=_cparams(("parallel",)),
        name="router",
    )(h, wp)


def _route(ids, tm, n_tiles):
    T = ids.shape[0]
    e = ids[:, :2].reshape(-1)
    onehot = (e[:, None] == jnp.arange(N_EXPERTS, dtype=jnp.int32)[None, :]).astype(jnp.int32)
    pos = jnp.take_along_axis(jnp.cumsum(onehot, axis=0), e[:, None], axis=1)[:, 0] - 1
    counts = jnp.sum(onehot, axis=0)
    padded = (counts + tm - 1) // tm * tm
    ends = jnp.cumsum(padded)
    dest = (ends - padded)[e] + pos
    row_token = jnp.zeros((n_tiles * tm,), jnp.int32).at[dest].set(
        jnp.arange(2 * T, dtype=jnp.int32) // 2)
    tile_start = jnp.arange(n_tiles, dtype=jnp.int32) * tm
    tile_expert = jnp.minimum(
        jnp.sum((ends[None, :] <= tile_start[:, None]).astype(jnp.int32), axis=1), N_EXPERTS - 1)
    n_used = (ends[-1] // tm).astype(jnp.int32).reshape(1)
    return row_token.reshape(n_tiles, 1, tm), dest, tile_expert, n_used


def _row_copy(src_ref, buf_ref, sem, src_row, dst_row):
    return pltpu.make_async_copy(src_ref.at[pl.ds(src_row, 1), :],
                                 buf_ref.at[pl.ds(dst_row, 1), :], sem.at[0])


def _gather_kernel(idx_ref, src_ref, o_ref, buf_ref, sem):
    tm = buf_ref.shape[0]

    def issue(g, c):
        for u in range(DMA_UNROLL):
            r = g * DMA_UNROLL + u
            _row_copy(src_ref, buf_ref, sem, idx_ref[0, 0, r], r).start(priority=u % 2)
        return c

    def drain(r, c):
        _row_copy(src_ref, buf_ref, sem, 0, r).wait()
        return c

    lax.fori_loop(0, tm // DMA_UNROLL, issue, 0)
    lax.fori_loop(0, tm, drain, 0, unroll=DMA_UNROLL)
    o_ref[...] = buf_ref[...].astype(o_ref.dtype)


def _gather_rows(src, row_token):
    n_tiles, _, tm = row_token.shape
    D = src.shape[1]
    return pl.pallas_call(
        _gather_kernel,
        grid=(n_tiles,),
        in_specs=[pl.BlockSpec((1, 1, tm), lambda i: (i, 0, 0), memory_space=pltpu.SMEM),
                  pl.BlockSpec(memory_space=pl.ANY)],
        out_specs=pl.BlockSpec((tm, D), lambda i: (i, 0)),
        out_shape=jax.ShapeDtypeStruct((n_tiles * tm, D), BF16),
        scratch_shapes=[pltpu.VMEM((tm, D), F32), pltpu.SemaphoreType.DMA((1,))],
        compiler_params=_cparams(("arbitrary",)),
        name="moe_gather",
    )(row_token, src)


def _moe_ffn_kernel(te_ref, nu_ref, x_ref, w1_ref, w3_ref, w2_ref, o_ref):
    j = pl.program_id(1)

    @pl.when(j == 0)
    def _():
        o_ref[...] = jnp.zeros_like(o_ref)

    @pl.when(pl.program_id(0) < nu_ref[0])
    def _():
        _swiglu_accumulate(x_ref, w1_ref[0], w3_ref[0], w2_ref[0], o_ref)


def _moe_ffn(xs, w1, w3, w2, tile_expert, n_used, tm):
    NP, D = xs.shape
    FF = w1.shape[2]
    bf = _tile(FF, 512, LANES)
    grid_spec = pltpu.PrefetchScalarGridSpec(
        num_scalar_prefetch=2,
        grid=(NP // tm, FF // bf),
        in_specs=[pl.BlockSpec((tm, D), lambda i, j, te, nu: (i, 0)),
                  pl.BlockSpec((1, D, bf), lambda i, j, te, nu: (te[i], 0, j)),
                  pl.BlockSpec((1, D, bf), lambda i, j, te, nu: (te[i], 0, j)),
                  pl.BlockSpec((1, bf, D), lambda i, j, te, nu: (te[i], j, 0))],
        out_specs=pl.BlockSpec((tm, D), lambda i, j, te, nu: (i, 0)),
    )
    return pl.pallas_call(
        _moe_ffn_kernel,
        grid_spec=grid_spec,
        out_shape=jax.ShapeDtypeStruct((NP, D), F32),
        compiler_params=_cparams(("parallel", "arbitrary")),
        name="moe_ffn",
    )(tile_expert, n_used, xs, w1, w3, w2)


def _combine_kernel(d_ref, y_ref, gw_ref, h_ref, g_ref, b_ref, o_ref, ob_ref, buf_ref, sem):
    bm = buf_ref.shape[1]

    def issue(g, c):
        for u in range(DMA_UNROLL // 2):
            t = g * (DMA_UNROLL // 2) + u
            for r in range(2):
                _row_copy(y_ref, buf_ref.at[r], sem, d_ref[0, 0, 2 * t + r], t).start(priority=r)
        return c

    def drain(t, c):
        for r in range(2):
            _row_copy(y_ref, buf_ref.at[r], sem, 0, t).wait()
        return c

    lax.fori_loop(0, bm // (DMA_UNROLL // 2), issue, 0)
    lax.fori_loop(0, bm, drain, 0, unroll=DMA_UNROLL // 2)
    gw = gw_ref[...]
    ff = gw[:, 0:1] * buf_ref[0] + gw[:, 1:2] * buf_ref[1]
    y = _layer_norm(DEEPNORM_ALPHA * h_ref[...] + ff, g_ref[...], b_ref[...])
    o_ref[...] = y
    ob_ref[...] = y.astype(BF16)


def _combine(ys, dest, gw, h, g, b):
    T, D = h.shape
    bm = _tile(T, 256)
    row = lambda w: pl.BlockSpec((bm, w), lambda i: (i, 0))
    vec = pl.BlockSpec((1, D), lambda i: (0, 0))
    return pl.pallas_call(
        _combine_kernel,
        grid=(T // bm,),
        in_specs=[pl.BlockSpec((1, 1, 2 * bm), lambda i: (i, 0, 0), memory_space=pltpu.SMEM),
                  pl.BlockSpec(memory_space=pl.ANY), row(LANES), row(D), vec, vec],
        out_specs=[row(D), row(D)],
        out_shape=[jax.ShapeDtypeStruct((T, D), F32), jax.ShapeDtypeStruct((T, D), BF16)],
        scratch_shapes=[pltpu.VMEM((2, bm, D), F32), pltpu.SemaphoreType.DMA((1,))],
        compiler_params=_cparams(("arbitrary",)),
        name="moe_combine",
    )(dest.reshape(T // bm, 1, 2 * bm), ys, gw, h, g.reshape(1, D), b.reshape(1, D))


def _swiglu_accumulate(x_ref, w1, w3, w2, acc_ref):
    hm = x_ref.shape[0] // 2
    rows = [slice(0, hm), slice(hm, 2 * hm)]
    ac = [(_dot(x_ref[r, :], w1), _dot(x_ref[r, :], w3)) for r in rows]
    hh = [(a * _sigmoid(a) * c).astype(BF16) for a, c in ac]
    for r, t in zip(rows, hh):
        acc_ref[r, :] += _dot(t, w2)


def _ffn_kernel(x_ref, w1_ref, w3_ref, w2_ref, h_ref, g_ref, b_ref, o_ref, ob_ref, acc_ref):
    j = pl.program_id(1)

    @pl.when(j == 0)
    def _():
        acc_ref[...] = jnp.zeros_like(acc_ref)

    _swiglu_accumulate(x_ref, w1_ref[...], w3_ref[...], w2_ref[...], acc_ref)

    @pl.when(j == pl.num_programs(1) - 1)
    def _():
        y = _layer_norm(DEEPNORM_ALPHA * h_ref[...] + acc_ref[...], g_ref[...], b_ref[...])
        o_ref[...] = y
        ob_ref[...] = y.astype(BF16)


def _ffn(xb, w1, w3, w2, h, g, b):
    M, D = xb.shape
    FF = w1.shape[1]
    bm = _tile(M, 512)
    bf = _tile(FF, 512, LANES)
    row = lambda w: pl.BlockSpec((bm, w), lambda i, j: (i, 0))
    vec = pl.BlockSpec((1, D), lambda i, j: (0, 0))
    return pl.pallas_call(
        _ffn_kernel,
        grid=(M // bm, FF // bf),
        in_specs=[row(D),
                  pl.BlockSpec((D, bf), lambda i, j: (0, j)),
                  pl.BlockSpec((D, bf), lambda i, j: (0, j)),
                  pl.BlockSpec((bf, D), lambda i, j: (j, 0)),
                  row(D), vec, vec],
        out_specs=[row(D), row(D)],
        out_shape=[jax.ShapeDtypeStruct((M, D), F32), jax.ShapeDtypeStruct((M, D), BF16)],
        scratch_shapes=[pltpu.VMEM((bm, D), F32)],
        compiler_params=_cparams(("parallel", "arbitrary")),
        name="ffn",
    )(xb, w1, w3, w2, h, g.reshape(1, D), b.reshape(1, D))


RWKV_LORA_W = 512
RWKV_SUM_W = 256


def _shift_lerp(ref, c0, width, r0, p0, notfirst, mu, row0):
    cur = ref[0, pl.ds(r0, CHUNK), c0:c0 + width].astype(F32)
    prev = ref[0, pl.ds(p0, BF16_ROWS), c0:c0 + width][BF16_ROWS - 1:BF16_ROWS, :].astype(F32)
    zprev = jnp.where(row0, prev * notfirst, pltpu.roll(cur, 1, 0))
    return cur + mu * (zprev - cur)


def _split3(x):
    hi = x.astype(BF16)
    r1 = x - hi.astype(F32)
    mid = r1.astype(BF16)
    lo = (r1 - mid.astype(F32)).astype(BF16)
    return hi, mid, lo


def _rwkv_kernel(z_ref, vec_ref, mul_ref, w2_ref, a2_ref, g2_ref, o_ref, s_ref):
    C, N = CHUNK, RWKV_HEAD_DIM
    W = o_ref.shape[2]
    H = W // N
    LW = RWKV_LORA_W
    nchunk = z_ref.shape[1] // C
    s_ref[...] = jnp.zeros_like(s_ref)

    ri = _iota((C, C), 0)
    ci = _iota((C, C), 1)
    strict = ri > ci
    incl = ri >= ci
    tri = incl.astype(BF16)
    eye = (ri == ci).astype(F32)
    SW = RWKV_SUM_W
    bd = (_iota((SW, SW), 0) // N == _iota((SW, SW), 1) // N).astype(BF16)
    row0_w = _iota((C, W), 0) == 0
    row0_l = _iota((C, LW), 0) == 0

    def head_sum(x):
        xb = x.astype(BF16)
        return jnp.concatenate([_dot(xb[:, g:g + SW], bd) for g in range(0, W, SW)], axis=1)

    w0 = vec_ref[0:1, :]
    a0 = vec_ref[1:2, :]
    k_k = vec_ref[2:3, :]
    k_a = vec_ref[3:4, :]
    r_k = vec_ref[4:5, :]
    gn_g = vec_ref[5:6, :]
    gn_b = vec_ref[6:7, :]
    mu_r = vec_ref[8:9, :]
    mu_k = vec_ref[9:10, :]
    mu_v = vec_ref[10:11, :]
    mu_l = mul_ref[0:1, :]

    def body(c, carry):
        r0 = pl.multiple_of(c * C, C)
        p0 = pl.multiple_of(jnp.maximum(r0 - BF16_ROWS, 0), BF16_ROWS)
        notfirst = jnp.where(c > 0, 1.0, 0.0).astype(F32)
        lo = _shift_lerp(z_ref, 0, LW, r0, p0, notfirst, mu_l, row0_l)
        r = _shift_lerp(z_ref, LW, W, r0, p0, notfirst, mu_r, row0_w)
        k = _shift_lerp(z_ref, LW + W, W, r0, p0, notfirst, mu_k, row0_w)
        v = _shift_lerp(z_ref, LW + 2 * W, W, r0, p0, notfirst, mu_v, row0_w)

        wd = lo[:, 0:128]
        ad = lo[:, 128:256]
        gd = lo[:, 256:512]
        w_log = -_softplus(-(w0 + _dot(jnp.tanh(wd).astype(BF16), w2_ref[...]))) - 0.5
        lw = -jnp.exp(w_log)
        a = _sigmoid(a0 + _dot(ad.astype(BF16), a2_ref[...]))
        g = _dot(_sigmoid(gd).astype(BF16), g2_ref[...])

        kk = k * k_k
        kk = kk / jnp.maximum(jnp.sqrt(head_sum(kk * kk)), 1e-12)
        kmod = k * (1.0 + (a - 1.0) * k_a)
        b = kk * a

        cum = sum(_dot(tri, part) for part in _split3(lw))
        cum_c = cum[C - 1:C, :]
        e_inv = jnp.exp(-cum)
        e_fin = jnp.exp(cum_c - cum)
        kkt = kk * jnp.exp(cum - lw)
        rt = r * jnp.exp(cum)
        bt = b * e_inv
        kt = kmod * e_inv
        bh = b * e_fin
        kh = kmod * e_fin
        pc = jnp.exp(cum_c)

        hs = range(H)
        sl = [slice(h * N, (h + 1) * N) for h in hs]
        cat = lambda top, bot, h: jnp.concatenate([top[:, sl[h]], bot[:, sl[h]]], axis=0).astype(BF16)
        lhs = [cat(kkt, rt, h) for h in hs]
        rhs = [cat(bt, kt, h) for h in hs]
        vb = [v[:, sl[h]].astype(BF16) for h in hs]
        s_old = [s_ref[h] for h in hs]
        mall = [_dot_nt(lhs[h], rhs[h]) for h in hs]
        gm = [_dot_nt(lhs[h], s_old[h].astype(BF16)) for h in hs]
        mb = [jnp.where(incl, mall[h][C:, :C], 0.0).astype(BF16) for h in hs]
        akmk = [jnp.concatenate([jnp.where(strict, mall[h][:C, C:], 0.0),
                                 jnp.where(incl, mall[h][C:, C:], 0.0)], axis=0).astype(BF16)
                for h in hs]
        av = [_dot(akmk[h], vb[h]) for h in hs]
        square = lambda xs: [_dot(xs[h], xs[h]).astype(BF16) for h in hs]
        x0 = [jnp.where(strict, -mall[h][:C, :C], 0.0) for h in hs]
        t = [eye + x0[h] for h in hs]
        xp = square([x0[h].astype(BF16) for h in hs])
        for level in range(5):
            xn = square(xp) if level < 4 else None
            t = [t[h] + _dot(t[h].astype(BF16), xp[h]) for h in hs]
            xp = xn
        u = [_dot(t[h].astype(BF16), (gm[h][:C] + av[h][:C]).astype(BF16)) for h in hs]
        o = [gm[h][C:] + av[h][C:] - _dot(mb[h], u[h].astype(BF16)) for h in hs]
        upd = [_dot_tn(jnp.concatenate([u[h].astype(BF16), vb[h]], axis=0), cat(-bh, kh, h))
               for h in hs]
        for h in hs:
            s_ref[h] = s_old[h] * pc[:, sl[h]] + upd[h]

        o = jnp.concatenate(o, axis=1)
        d = o - head_sum(o) * (1.0 / N)
        var = head_sum(d * d) * (1.0 / N)
        on = d * lax.rsqrt(var + RWKV_GN_EPS) * gn_g + gn_b
        bonus = head_sum(r * kmod * r_k) * v
        o_ref[0, pl.ds(r0, C), :] = ((on + bonus) * g).astype(o_ref.dtype)
        return carry

    lax.fori_loop(0, nchunk, body, 0)


def _rwkv(z, vecs, mul, w2, a2, g2, B, LP):
    W = vecs.shape[1]
    ZW = z.shape[2]
    whole = lambda shape: pl.BlockSpec(shape, lambda b: (0,) * len(shape))
    return pl.pallas_call(
        _rwkv_kernel,
        grid=(B,),
        in_specs=[pl.BlockSpec((1, LP, ZW), lambda b: (b, 0, 0)),
                  whole((16, W)), whole((8, RWKV_LORA_W)), whole((LANES, W)), whole((LANES, W)),
                  whole((RWKV_GATE_RANK, W))],
        out_specs=pl.BlockSpec((1, LP, W), lambda b: (b, 0, 0)),
        out_shape=jax.ShapeDtypeStruct((B, LP, W), BF16),
        scratch_shapes=[pltpu.VMEM((W // RWKV_HEAD_DIM, RWKV_HEAD_DIM, RWKV_HEAD_DIM), F32)],
        compiler_params=_cparams(("parallel",)),
        name="rwkv7",
    )(z, vecs, mul, w2, a2, g2)


def _fox_cum_kernel(zs_ref, bias_ref, o_ref):
    LP = zs_ref.shape[1]
    nkb = o_ref.shape[2]
    TB = ATT_BLOCK
    o_ref[...] = jnp.zeros_like(o_ref)
    sel = (_iota((16, LANES), 0) == _iota((16, LANES), 1)).astype(F32)
    carry = jnp.zeros((16, 1), F32)
    for kb in range(nkb):
        n = min(TB, LP - kb * TB)
        lf = _log_sigmoid(zs_ref[0, kb * TB:kb * TB + n, :] + bias_ref[...])
        lft = _dot_nt(sel, lf, HI)
        upper = (_iota((n, n), 0) <= _iota((n, n), 1)).astype(F32)
        cs = _dot(lft, upper, HI) + carry
        carry = cs[:, n - 1:n]
        for hp in range(o_ref.shape[1]):
            o_ref[0, hp, kb, 0:2, 0:n] = cs[2 * hp:2 * hp + 2, :]


def _fox_cum(zs, bias, B, LP, n_pairs):
    nkb = pl.cdiv(LP, ATT_BLOCK)
    return pl.pallas_call(
        _fox_cum_kernel,
        grid=(B,),
        in_specs=[pl.BlockSpec((1, LP, LANES), lambda b: (b, 0, 0)),
                  pl.BlockSpec((1, LANES), lambda b: (0, 0))],
        out_specs=pl.BlockSpec((1, n_pairs, nkb, SUBLANES, ATT_BLOCK), lambda b: (b, 0, 0, 0, 0)),
        out_shape=jax.ShapeDtypeStruct((B, n_pairs, nkb, SUBLANES, ATT_BLOCK), F32),
        compiler_params=_cparams(("parallel",)),
        name="fox_cum",
    )(zs, bias)


def _fox_kernel(q_ref, k_ref, v_ref, ct_ref, o_ref):
    TB, N = ATT_BLOCK, FOX_HEAD_DIM
    LP = k_ref.shape[1]
    n_full = LP // TB
    tail = LP - n_full * TB
    qi = pl.program_id(2)

    def attend(rows):
        halves = 2 if rows == TB else 1
        HQ = rows // halves
        units = [(h, s) for h in range(2) for s in range(halves)]
        nu = len(units)
        lane_q = _iota((HQ, LANES), 1)
        scale = N ** -0.5
        qu = [jnp.where((lane_q < N) == (h == 0), q_ref[0, s * HQ:(s + 1) * HQ, :] * scale, 0.0)
              for h, s in units]

        def block(k0, n, cks, masks, state):
            kblk = k_ref[0, pl.ds(k0, n), :]
            vblk = v_ref[0, pl.ds(k0, n), :]
            lane_v = _iota((n, LANES), 1)
            vaug = [jnp.where((lane_v < N) == (h == 0), vblk, 1.0) for h in range(2)]
            sc = [_dot_nt(qu[i], kblk) - cks[h] for i, (h, s) in enumerate(units)]
            if masks is not None:
                sc = [jnp.where(masks[s], sc[i], -jnp.inf) for i, (h, s) in enumerate(units)]
            m_new = [jnp.maximum(state[i][0], jnp.max(sc[i], axis=-1, keepdims=True))
                     for i in range(nu)]
            p = [jnp.exp(sc[i] - m_new[i]).astype(BF16) for i in range(nu)]
            pv = [_dot(p[i], vaug[h]) for i, (h, s) in enumerate(units)]
            return tuple((m_new[i], jnp.exp(state[i][0] - m_new[i]) * state[i][1] + pv[i])
                         for i in range(nu))

        ck = lambda kb, h: ct_ref[0, 0, kb, h:h + 1, :]

        def pair_body(kp, state):
            cks = [jnp.concatenate([ck(2 * kp, h), ck(2 * kp + 1, h)], axis=1) for h in range(2)]
            return block(pl.multiple_of(kp * 2 * TB, 2 * TB), 2 * TB, cks, None, state)

        def single_body(_, state):
            kb = qi - 1
            return block(pl.multiple_of(kb * TB, TB), TB, [ck(kb, h) for h in range(2)], None,
                         state)

        state = tuple((jnp.full((HQ, 1), NEG_BIG, F32), jnp.zeros((HQ, LANES), F32))
                      for _ in units)
        state = lax.fori_loop(0, lax.shift_right_logical(qi, 1), pair_body, state)
        state = lax.fori_loop(0, qi & 1, single_body, state)
        cks = [ck(qi, h)[:, 0:rows] for h in range(2)]
        masks = [_iota((HQ, rows), 0) + s * HQ >= _iota((HQ, rows), 1) for s in range(halves)]
        final = block(pl.multiple_of(qi * TB, TB), rows, cks, masks, state)
        for s in range(halves):
            acc0, acc1 = final[s][1], final[halves + s][1]
            out = jnp.where(lane_q < N, acc0 / acc0[:, N:N + 1], acc1 / acc1[:, 0:1])
            o_ref[0, s * HQ:(s + 1) * HQ, :] = out.astype(o_ref.dtype)

    if tail == 0:
        attend(TB)
    else:
        pl.when(qi < n_full)(lambda: attend(TB))
        pl.when(qi == n_full)(lambda: attend(tail))


def _fox(zq, ct, B, LP, width):
    n_pairs = width // LANES
    nq = pl.cdiv(LP, ATT_BLOCK)
    nkb = ct.shape[2]
    kv = lambda off: pl.BlockSpec((1, LP, LANES), lambda b, p, i: (b, 0, off + p))
    return pl.pallas_call(
        _fox_kernel,
        grid=(B, n_pairs, nq),
        in_specs=[pl.BlockSpec((1, ATT_BLOCK, LANES), lambda b, p, i: (b, i, p)),
                  kv(n_pairs), kv(2 * n_pairs),
                  pl.BlockSpec((1, 1, nkb, SUBLANES, ATT_BLOCK), lambda b, p, i: (b, p, 0, 0, 0))],
        out_specs=pl.BlockSpec((1, ATT_BLOCK, LANES), lambda b, p, i: (b, i, p)),
        out_shape=jax.ShapeDtypeStruct((B, LP, width), BF16),
        compiler_params=_cparams(("parallel", "parallel", "arbitrary")),
        name="fox_attention",
    )(zq, zq, zq, ct)


MLSTM_GATE_LANE_I = 12
MLSTM_GATE_LANE_F = 16


def _conv_silu(ref, r0, p0, notfirst, w_ref):
    cur = ref[0, pl.ds(r0, CHUNK), :].astype(F32)
    prev = ref[0, pl.ds(p0, BF16_ROWS), :][BF16_ROWS - SUBLANES:, :].astype(F32) * notfirst
    row8 = _iota(prev.shape, 0)
    y = cur * w_ref[MLSTM_CONV - 1:MLSTM_CONV, :]
    for s in range(1, MLSTM_CONV):
        rolled = pltpu.roll(cur, s, 0)
        top = jnp.where(row8 < s, pltpu.roll(prev, s, 0), rolled[0:SUBLANES])
        shifted = jnp.concatenate([top, rolled[SUBLANES:]], axis=0)
        y = y + shifted * w_ref[MLSTM_CONV - 1 - s:MLSTM_CONV - s, :]
    return y * _sigmoid(y)


def _mlstm_kernel(q_ref, k_ref, v_ref, og_ref, zs_ref, bias_ref, cwq_ref, cwk_ref,
                  o_ref, c_ref, m_ref):
    C, H = CHUNK, MLSTM_HEADS
    Dh = q_ref.shape[2] // H
    nchunk = q_ref.shape[1] // C
    c_ref[...] = jnp.zeros_like(c_ref)
    m_ref[...] = jnp.zeros_like(m_ref)

    ri = _iota((C, C), 0)
    ci = _iota((C, C), 1)
    incl = ri >= ci
    tri = incl.astype(F32)
    upper = (ri <= ci).astype(F32)
    sel = (_iota((SUBLANES, LANES), 1) == _iota((SUBLANES, LANES), 0) + MLSTM_GATE_LANE_I).astype(F32)
    ones_col = (_iota((C, LANES), 1) == 0).astype(BF16)
    lane = _iota((C, LANES), 1)

    def body(c, carry):
        r0 = pl.multiple_of(c * C, C)
        p0 = pl.multiple_of(jnp.maximum(r0 - BF16_ROWS, 0), BF16_ROWS)
        notfirst = jnp.where(c > 0, 1.0, 0.0).astype(F32)
        q = _conv_silu(q_ref, r0, p0, notfirst, cwq_ref)
        k = _conv_silu(k_ref, r0, p0, notfirst, cwk_ref) * (Dh ** -0.5)
        v = v_ref[0, pl.ds(r0, C), :]
        og = og_ref[0, pl.ds(r0, C), :].astype(F32)
        gz = zs_ref[0, pl.ds(r0, C), :] + bias_ref[...]
        x = jnp.where(lane < MLSTM_GATE_LANE_F, gz, _log_sigmoid(gz))
        xt = _dot_nt(sel, x, HI)
        xc = _dot_nt(x, sel, HI)
        bt = _dot(xt, upper, HI)
        bc = _dot(tri, xc, HI)

        hs = range(H)
        sl = [slice(h * Dh, (h + 1) * Dh) for h in hs]
        m_all = m_ref[...]
        m_old = [m_all[h:h + 1, 0:1] for h in hs]
        c_old = [c_ref[h] for h in hs]
        li_r = [xt[h:h + 1, :] for h in hs]
        b_r = [bt[H + h:H + h + 1, :] for h in hs]
        li_c = [xc[:, h:h + 1] for h in hs]
        b_c = [bc[:, H + h:H + h + 1] for h in hs]
        qh = [q[:, sl[h]].astype(BF16) for h in hs]
        kh = [k[:, sl[h]] for h in hs]
        vaug = [jnp.concatenate([v[:, sl[h]], ones_col], axis=1) for h in hs]
        qk = [_dot_nt(qh[h], kh[h].astype(BF16)) for h in hs]
        qc = [_dot(qh[h], c_old[h].astype(BF16)) for h in hs]
        log_d = [jnp.where(incl, b_c[h] - b_r[h] + li_r[h], -jnp.inf) for h in hs]
        log_inter = [b_c[h] + m_old[h] for h in hs]
        m_q = [jnp.maximum(log_inter[h], jnp.max(log_d[h], axis=-1, keepdims=True)) for h in hs]
        s = [(qk[h] * jnp.exp(log_d[h] - m_q[h])).astype(BF16) for h in hs]
        b_end = [b_c[h][C - 1:C, :] for h in hs]
        m_new = [jnp.maximum(b_end[h] + m_old[h],
                             jnp.max(b_end[h] - b_r[h] + li_r[h], axis=-1, keepdims=True))
                 for h in hs]
        gk = [jnp.exp(b_end[h] - b_c[h] + li_c[h] - m_new[h]) for h in hs]
        num = [jnp.exp(log_inter[h] - m_q[h]) * qc[h] + _dot(s[h], vaug[h]) for h in hs]
        kv = [_dot_tn((gk[h] * kh[h]).astype(BF16), vaug[h]) for h in hs]
        outs = [_sigmoid(og[:, sl[h]]) * num[h][:, :Dh]
                / jnp.maximum(jnp.abs(num[h][:, Dh:Dh + 1]), jnp.exp(-m_q[h])) for h in hs]
        o_ref[0, pl.ds(r0, C), :] = jnp.concatenate(outs, axis=1).astype(o_ref.dtype)
        for h in hs:
            c_ref[h] = jnp.exp(b_end[h] + m_old[h] - m_new[h]) * c_old[h] + kv[h]
        m_ref[0:H, :] = jnp.concatenate([jnp.broadcast_to(m_new[h], (1, LANES)) for h in hs],
                                        axis=0)
        return carry

    lax.fori_loop(0, nchunk, body, 0)


def _mlstm(zm, zs, bias, cwq, cwk, B, LP):
    W = zm.shape[2] // 4
    Dh = W // MLSTM_HEADS
    seq = lambda j: pl.BlockSpec((1, LP, W), lambda b: (b, 0, j))
    cw = pl.BlockSpec((SUBLANES, W), lambda b: (0, 0))
    return pl.pallas_call(
        _mlstm_kernel,
        grid=(B,),
        in_specs=[seq(0), seq(1), seq(2), seq(3),
                  pl.BlockSpec((1, LP, LANES), lambda b: (b, 0, 0)),
                  pl.BlockSpec((1, LANES), lambda b: (0, 0)), cw, cw],
        out_specs=pl.BlockSpec((1, LP, W), lambda b: (b, 0, 0)),
        out_shape=jax.ShapeDtypeStruct((B, LP, W), BF16),
        scratch_shapes=[pltpu.VMEM((MLSTM_HEADS, Dh, 2 * Dh), F32),
                        pltpu.VMEM((SUBLANES, LANES), F32)],
        compiler_params=_cparams(("parallel",)),
        name="mlstm",
    )(zm, zm, zm, zm, zs, bias, cwq, cwk)


def _pad_rows(w, n):
    return jnp.pad(w, ((0, n - w.shape[0]), (0, 0)))


def kernel(x, meta_tokens, ln_emb_g, ln_emb_b, w_in, rwkv_mu, rwkv_w0, rwkv_w2, rwkv_a0, rwkv_a2, rwkv_g2, rwkv_k_k, rwkv_k_a, rwkv_r_k, rwkv_gn_g, rwkv_gn_b, fox_b_f, mlstm_conv_w, mlstm_b_i, mlstm_b_f, proj_rwkv, proj_fox, proj_mlstm, w_out, ln1_g, ln1_b, ffn_w1, ffn_w3, ffn_w2, router_w, moe_w1, moe_w3, moe_w2, ln2_g, ln2_b):
    B, S, D = x.shape
    L = N_META + S
    LP = -(-L // CHUNK) * CHUNK
    T = B * LP
    RW = proj_rwkv.shape[1]
    FW = proj_fox.shape[1]
    MW = proj_mlstm.shape[1]
    n_fox_heads = FW // FOX_HEAD_DIM
    dr, ar, gr = RWKV_DECAY_RANK, RWKV_AAA_RANK, RWKV_GATE_RANK
    c1 = 3 * RW + dr + ar + gr

    meta = jnp.broadcast_to(meta_tokens[None].astype(x.dtype), (B, N_META, D))
    hin = jnp.concatenate([meta, x, jnp.zeros((B, LP - L, D), x.dtype)], axis=1).reshape(T, D)
    h, hb = _ln(hin, ln_emb_g, ln_emb_b)

    for l in range(DEPTH):
        w_r, w_f, w_m, w_s, w_g = _prep_w_in(w_in, l, RW, FW, MW, dr, ar, gr)

        z_r = _mm(hb, w_r, BF16, 1408, "in_rwkv").reshape(B, LP, -1)
        z_f = _mm(hb, w_f, BF16, 1152, "in_fox").reshape(B, LP, -1)
        z_m = _mm(hb, w_m, BF16, 1024, "in_mlstm").reshape(B, LP, -1)
        z_s = _mm(hb, w_s, F32, LANES, "in_gates").reshape(B, LP, LANES)

        mu = rwkv_mu[l]
        vecs = jnp.zeros((16, RW), F32)
        for i, p in enumerate([rwkv_w0[l], rwkv_a0[l], rwkv_k_k[l], rwkv_k_a[l],
                               rwkv_r_k[l].reshape(RW), rwkv_gn_g[l], rwkv_gn_b[l]]):
            vecs = vecs.at[i].set(p)
        vecs = vecs.at[8].set(mu[:RW]).at[9].set(mu[RW:2 * RW]).at[10].set(mu[2 * RW:3 * RW])
        mul = jnp.zeros((8, RWKV_LORA_W), F32)
        mul = mul.at[0, 0:dr].set(mu[3 * RW:3 * RW + dr])
        mul = mul.at[0, LANES:LANES + ar].set(mu[3 * RW + dr:3 * RW + dr + ar])
        mul = mul.at[0, 2 * LANES:2 * LANES + gr].set(mu[3 * RW + dr + ar:c1])
        y_r = _rwkv(z_r, vecs, mul, _pad_rows(rwkv_w2[l], LANES).astype(BF16),
                    _pad_rows(rwkv_a2[l], LANES).astype(BF16), rwkv_g2[l].astype(BF16), B, LP)

        bias = jnp.zeros((1, LANES), F32)
        bias = bias.at[0, 0:n_fox_heads].set(fox_b_f[l])
        bias = bias.at[0, MLSTM_GATE_LANE_I:MLSTM_GATE_LANE_I + MLSTM_HEADS].set(mlstm_b_i[l])
        bias = bias.at[0, MLSTM_GATE_LANE_F:MLSTM_GATE_LANE_F + MLSTM_HEADS].set(mlstm_b_f[l])

        ct = _fox_cum(z_s, bias, B, LP, FW // LANES)
        y_f = _fox(z_f, ct, B, LP, FW)

        cw = _pad_rows(mlstm_conv_w[l], SUBLANES)
        y_m = _mlstm(z_m, z_s, bias, cw[:, :MW], cw[:, MW:], B, LP)

        pre = _merge(hb, w_g, y_r.reshape(T, RW), y_f.reshape(T, FW), y_m.reshape(T, MW),
                     proj_rwkv[l].astype(BF16), proj_fox[l].astype(BF16),
                     proj_mlstm[l].astype(BF16))
        h, hb = _mm_res_ln(pre, w_out[l].astype(BF16), h, ln1_g[l], ln1_b[l])

        i = l // 2
        if l % 2 == 0:
            h, hb = _ffn(hb, ffn_w1[i].astype(BF16), ffn_w3[i].astype(BF16),
                         ffn_w2[i].astype(BF16), h, ln2_g[l], ln2_b[l])
        else:
            tm = _tile(T, MOE_TILE)
            n_tiles = 2 * T // tm + N_EXPERTS
            gw, ids = _router(h, router_w[i])
            row_token, dest, tile_expert, n_used = _route(ids, tm, n_tiles)
            xs = _gather_rows(h, row_token)
            ys = _moe_ffn(xs, moe_w1[i].astype(BF16), moe_w3[i].astype(BF16),
                          moe_w2[i].astype(BF16), tile_expert, n_used, tm)
            h, hb = _combine(ys, dest, gw, h, ln2_g[l], ln2_b[l])

    return h.reshape(B, LP, D)[:, N_META:L]
```

```python
import functools
import math

import jax
import jax.numpy as jnp
from jax import lax
from jax.experimental import pallas as pl
from jax.experimental.pallas import tpu as pltpu

F32 = jnp.float32
BF16 = jnp.bfloat16
HI = lax.Precision.HIGHEST

N_META = 16
DEPTH = 4
RWKV_HEAD_DIM = 64
RWKV_DECAY_RANK = 96
RWKV_AAA_RANK = 96
RWKV_GATE_RANK = 256
RWKV_GN_EPS = 64e-5
FOX_HEAD_DIM = 64
MLSTM_HEADS = 4
MLSTM_CONV = 4
N_EXPERTS = 8
LN_EPS = 1e-5
DEEPNORM_ALPHA = (2 * DEPTH) ** 0.25

CHUNK = 64
LANES = 128
SUBLANES = 8
BF16_ROWS = 16
ATT_BLOCK = 256
MOE_TILE = 512
DMA_UNROLL = 8
VMEM_LIMIT = 56 * 1024 * 1024
NEG_BIG = -1e30


def _cparams(sem):
    return pltpu.CompilerParams(dimension_semantics=sem, vmem_limit_bytes=VMEM_LIMIT)


def _tile(n, pref, mult=SUBLANES):
    best = None
    for d in range(mult, min(n, pref) + 1, mult):
        if n % d == 0:
            best = d
    return best if best is not None else n


def _dot(a, b, precision=None):
    return jnp.dot(a, b, preferred_element_type=F32, precision=precision)


def _dot_nt(a, b, precision=None):
    return lax.dot_general(a, b, (((1,), (1,)), ((), ())), preferred_element_type=F32,
                           precision=precision)


def _dot_tn(a, b, precision=None):
    return lax.dot_general(a, b, (((0,), (0,)), ((), ())), preferred_element_type=F32,
                           precision=precision)


def _softplus(x):
    return jnp.maximum(x, 0.0) + jnp.log(1.0 + jnp.exp(-jnp.abs(x)))


def _log_sigmoid(x):
    return jnp.minimum(x, 0.0) - jnp.log(1.0 + jnp.exp(-jnp.abs(x)))


def _sigmoid(x):
    return 1.0 / (1.0 + jnp.exp(-x))


def _iota(shape, dim):
    return lax.broadcasted_iota(jnp.int32, shape, dim)


def _layer_norm(t, g, b):
    mu = jnp.mean(t, axis=-1, keepdims=True)
    d = t - mu
    var = jnp.mean(d * d, axis=-1, keepdims=True)
    return d * lax.rsqrt(var + LN_EPS) * g + b


def _ln_kernel(x_ref, g_ref, b_ref, o_ref, ob_ref):
    y = _layer_norm(x_ref[...], g_ref[...], b_ref[...])
    o_ref[...] = y
    ob_ref[...] = y.astype(BF16)


def _ln(x, g, b):
    T, D = x.shape
    bm = _tile(T, 512)
    return pl.pallas_call(
        _ln_kernel,
        grid=(T // bm,),
        in_specs=[pl.BlockSpec((bm, D), lambda i: (i, 0)),
                  pl.BlockSpec((1, D), lambda i: (0, 0)),
                  pl.BlockSpec((1, D), lambda i: (0, 0))],
        out_specs=[pl.BlockSpec((bm, D), lambda i: (i, 0)),
                   pl.BlockSpec((bm, D), lambda i: (i, 0))],
        out_shape=[jax.ShapeDtypeStruct((T, D), F32), jax.ShapeDtypeStruct((T, D), BF16)],
        compiler_params=_cparams(("parallel",)),
        name="ln_embed",
    )(x, g.reshape(1, D), b.reshape(1, D))


def _prep_w_in_kernel(w_ref, or_ref, of_ref, om_ref, os_ref, og_ref, *, rw, fw, mw, dr, ar):
    x = w_ref[0]
    c1 = or_ref.shape[1] - 2 * LANES + dr + ar
    c2 = c1 + 3 * fw + fw // FOX_HEAD_DIM
    c3 = c2 + 4 * mw + 2 * MLSTM_HEADS
    zeros = lambda n: jnp.zeros((x.shape[0], n), F32)
    cast = lambda parts: jnp.concatenate(parts, axis=1).astype(BF16)
    or_ref[...] = cast([x[:, 3 * rw:3 * rw + dr], zeros(LANES - dr),
                        x[:, 3 * rw + dr:3 * rw + dr + ar], zeros(LANES - ar),
                        x[:, 3 * rw + dr + ar:c1], x[:, 0:3 * rw]])
    of_ref[...] = x[:, c1:c1 + 3 * fw].astype(BF16)
    om_ref[...] = x[:, c2:c2 + 4 * mw].astype(BF16)
    n_small = c2 - (c1 + 3 * fw) + c3 - (c2 + 4 * mw)
    os_ref[...] = cast([x[:, c1 + 3 * fw:c2], x[:, c2 + 4 * mw:c3], zeros(LANES - n_small)])
    og_ref[...] = x[:, c3:].astype(BF16)


def _prep_w_in(w_in, l, rw, fw, mw, dr, ar, gr):
    _, D, n_in = w_in.shape
    rows = _tile(D, 128, BF16_ROWS)
    widths = [3 * rw + 2 * LANES + gr, 3 * fw, 4 * mw, LANES, 3 * D]
    return pl.pallas_call(
        functools.partial(_prep_w_in_kernel, rw=rw, fw=fw, mw=mw, dr=dr, ar=ar),
        grid=(D // rows,),
        in_specs=[pl.BlockSpec((1, rows, n_in), lambda i: (l, i, 0))],
        out_specs=[pl.BlockSpec((rows, w), lambda i: (i, 0)) for w in widths],
        out_shape=[jax.ShapeDtypeStruct((D, w), BF16) for w in widths],
        compiler_params=_cparams(("parallel",)),
        name="prep_w_in",
    )(w_in)


def _mm_kernel(a_ref, b_ref, o_ref):
    o_ref[...] = _dot(a_ref[...], b_ref[...]).astype(o_ref.dtype)


def _mm(a, b, out_dtype, bn_pref, name):
    M, K = a.shape
    N = b.shape[1]
    bm = _tile(M, 1024)
    bn = _tile(N, bn_pref, LANES)
    return pl.pallas_call(
        _mm_kernel,
        grid=(M // bm, N // bn),
        in_specs=[pl.BlockSpec((bm, K), lambda i, j: (i, 0)),
                  pl.BlockSpec((K, bn), lambda i, j: (0, j))],
        out_specs=pl.BlockSpec((bm, bn), lambda i, j: (i, j)),
        out_shape=jax.ShapeDtypeStruct((M, N), out_dtype),
        compiler_params=_cparams(("parallel", "arbitrary")),
        name=name,
    )(a, b)


def _mm_res_ln_kernel(a_ref, w_ref, h_ref, g_ref, b_ref, o_ref, ob_ref):
    t = DEEPNORM_ALPHA * h_ref[...] + _dot(a_ref[...], w_ref[...])
    y = _layer_norm(t, g_ref[...], b_ref[...])
    o_ref[...] = y
    ob_ref[...] = y.astype(BF16)


def _mm_res_ln(a, w, h, g, b):
    M, K = a.shape
    D = w.shape[1]
    bm = _tile(M, 512)
    return pl.pallas_call(
        _mm_res_ln_kernel,
        grid=(M // bm,),
        in_specs=[pl.BlockSpec((bm, K), lambda i: (i, 0)),
                  pl.BlockSpec((K, D), lambda i: (0, 0)),
                  pl.BlockSpec((bm, D), lambda i: (i, 0)),
                  pl.BlockSpec((1, D), lambda i: (0, 0)),
                  pl.BlockSpec((1, D), lambda i: (0, 0))],
        out_specs=[pl.BlockSpec((bm, D), lambda i: (i, 0)),
                   pl.BlockSpec((bm, D), lambda i: (i, 0))],
        out_shape=[jax.ShapeDtypeStruct((M, D), F32), jax.ShapeDtypeStruct((M, D), BF16)],
        compiler_params=_cparams(("parallel",)),
        name="out_proj_ln",
    )(a, w, h, g.reshape(1, D), b.reshape(1, D))


def _merge_kernel(hb_ref, wgr_ref, wgf_ref, wgm_ref, yr_ref, yf_ref, ym_ref, pr_ref, pf_ref,
                  pm_ref, o_ref):
    hb = hb_ref[...]
    acc = _sigmoid(_dot(hb, wgr_ref[...])) * _dot(yr_ref[...], pr_ref[...])
    acc += _sigmoid(_dot(hb, wgf_ref[...])) * _dot(yf_ref[...], pf_ref[...])
    acc += _sigmoid(_dot(hb, wgm_ref[...])) * _dot(ym_ref[...], pm_ref[...])
    o_ref[...] = acc.astype(o_ref.dtype)


def _merge(hb, wg, yr, yf, ym, pr, pf, pm):
    M, D = hb.shape
    bm = _tile(M, 1024)
    bn = _tile(D, 512, LANES)
    nb = D // bn
    row = lambda w: pl.BlockSpec((bm, w), lambda i, j: (i, 0))
    col = lambda k: pl.BlockSpec((k, bn), lambda i, j: (0, j))
    gate = lambda br: pl.BlockSpec((D, bn), lambda i, j: (0, br * nb + j))
    return pl.pallas_call(
        _merge_kernel,
        grid=(M // bm, nb),
        in_specs=[row(D), gate(0), gate(1), gate(2),
                  row(yr.shape[1]), row(yf.shape[1]), row(ym.shape[1]),
                  col(pr.shape[0]), col(pf.shape[0]), col(pm.shape[0])],
        out_specs=pl.BlockSpec((bm, bn), lambda i, j: (i, j)),
        out_shape=jax.ShapeDtypeStruct((M, D), BF16),
        compiler_params=_cparams(("parallel", "arbitrary")),
        name="merge",
    )(hb, wg, wg, wg, yr, yf, ym, pr, pf, pm)


def _router_kernel(h_ref, w_ref, ow_ref, oi_ref):
    logits = _dot(h_ref[...], w_ref[...], HI)
    lane = _iota(logits.shape, 1)
    logits = jnp.where(lane < N_EXPERTS, logits, -jnp.inf)
    m1 = jnp.max(logits, axis=-1, keepdims=True)
    i1 = jnp.min(jnp.where(logits == m1, lane, LANES), axis=-1, keepdims=True)
    rest = jnp.where(lane == i1, -jnp.inf, logits)
    m2 = jnp.max(rest, axis=-1, keepdims=True)
    i2 = jnp.min(jnp.where(rest == m2, lane, LANES), axis=-1, keepdims=True)
    e2 = jnp.exp(m2 - m1)
    den = 1.0 + e2
    ow_ref[...] = jnp.where(lane == 0, 1.0 / den, jnp.where(lane == 1, e2 / den, 0.0))
    oi_ref[...] = jnp.where(lane == 0, i1, jnp.where(lane == 1, i2, 0))


def _router(h, w):
    T, D = h.shape
    bm = _tile(T, 512)
    wp = jnp.zeros((D, LANES), F32).at[:, :N_EXPERTS].set(w)
    out = pl.BlockSpec((bm, LANES), lambda i: (i, 0))
    return pl.pallas_call(
        _router_kernel,
        grid=(T // bm,),
        in_specs=[pl.BlockSpec((bm, D), lambda i: (i, 0)),
                  pl.BlockSpec((D, LANES), lambda i: (0, 0))],
        out_specs=[out, out],
        out_shape=[jax.ShapeDtypeStruct((T, LANES), F32),
                   jax.ShapeDtypeStruct((T, LANES), jnp.int32)],
        compiler_params=_cparams(("parallel",)),
        name="router",
    )(h, wp)


def _route(ids, tm, n_tiles):
    T = ids.shape[0]
    e = ids[:, :2].reshape(-1)
    onehot = (e[:, None] == jnp.arange(N_EXPERTS, dtype=jnp.int32)[None, :]).astype(jnp.int32)
    pos = jnp.take_along_axis(jnp.cumsum(onehot, axis=0), e[:, None], axis=1)[:, 0] - 1
    counts = jnp.sum(onehot, axis=0)
    padded = (counts + tm - 1) // tm * tm
    ends = jnp.cumsum(padded)
    dest = (ends - padded)[e] + pos
    row_token = jnp.zeros((n_tiles * tm,), jnp.int32).at[dest].set(
        jnp.arange(2 * T, dtype=jnp.int32) // 2)
    tile_start = jnp.arange(n_tiles, dtype=jnp.int32) * tm
    tile_expert = jnp.minimum(
        jnp.sum((ends[None, :] <= tile_start[:, None]).astype(jnp.int32), axis=1), N_EXPERTS - 1)
    n_used = (ends[-1] // tm).astype(jnp.int32).reshape(1)
    return row_token.reshape(n_tiles, 1, tm), dest, tile_expert, n_used


def _row_copy(src_ref, buf_ref, sem, src_row, dst_row):
    return pltpu.make_async_copy(src_ref.at[pl.ds(src_row, 1), :],
                                 buf_ref.at[pl.ds(dst_row, 1), :], sem.at[0])


def _gather_kernel(idx_ref, src_ref, o_ref, buf_ref, sem):
    tm = buf_ref.shape[0]

    def issue(r, c):
        _row_copy(src_ref, buf_ref, sem, idx_ref[0, 0, r], r).start()
        return c

    def drain(r, c):
        _row_copy(src_ref, buf_ref, sem, 0, r).wait()
        return c

    lax.fori_loop(0, tm, issue, 0, unroll=DMA_UNROLL)
    lax.fori_loop(0, tm, drain, 0, unroll=DMA_UNROLL)
    o_ref[...] = buf_ref[...].astype(o_ref.dtype)


def _gather_rows(src, row_token):
    n_tiles, _, tm = row_token.shape
    D = src.shape[1]
    return pl.pallas_call(
        _gather_kernel,
        grid=(n_tiles,),
        in_specs=[pl.BlockSpec((1, 1, tm), lambda i: (i, 0, 0), memory_space=pltpu.SMEM),
                  pl.BlockSpec(memory_space=pl.ANY)],
        out_specs=pl.BlockSpec((tm, D), lambda i: (i, 0)),
        out_shape=jax.ShapeDtypeStruct((n_tiles * tm, D), BF16),
        scratch_shapes=[pltpu.VMEM((tm, D), F32), pltpu.SemaphoreType.DMA((1,))],
        compiler_params=_cparams(("arbitrary",)),
        name="moe_gather",
    )(row_token, src)


def _moe_ffn_kernel(te_ref, nu_ref, x_ref, w1_ref, w3_ref, w2_ref, o_ref):
    j = pl.program_id(1)

    @pl.when(j == 0)
    def _():
        o_ref[...] = jnp.zeros_like(o_ref)

    @pl.when(pl.program_id(0) < nu_ref[0])
    def _():
        _swiglu_accumulate(x_ref, w1_ref[0, 0], w3_ref[0, 0], w2_ref[0, 0], o_ref)


def _moe_ffn(xs, weights, layer, tile_expert, n_used, tm):
    w1, w3, w2 = weights
    NP, D = xs.shape
    FF = w1.shape[3]
    bf = _tile(FF, 512, LANES)
    grid_spec = pltpu.PrefetchScalarGridSpec(
        num_scalar_prefetch=2,
        grid=(NP // tm, FF // bf),
        in_specs=[pl.BlockSpec((tm, D), lambda i, j, te, nu: (i, 0)),
                  pl.BlockSpec((1, 1, D, bf), lambda i, j, te, nu: (layer, te[i], 0, j)),
                  pl.BlockSpec((1, 1, D, bf), lambda i, j, te, nu: (layer, te[i], 0, j)),
                  pl.BlockSpec((1, 1, bf, D), lambda i, j, te, nu: (layer, te[i], j, 0))],
        out_specs=pl.BlockSpec((tm, D), lambda i, j, te, nu: (i, 0)),
    )
    return pl.pallas_call(
        _moe_ffn_kernel,
        grid_spec=grid_spec,
        out_shape=jax.ShapeDtypeStruct((NP, D), F32),
        compiler_params=_cparams(("parallel", "arbitrary")),
        name="moe_ffn",
    )(tile_expert, n_used, xs, w1, w3, w2)


def _combine_kernel(d_ref, y_ref, gw_ref, h_ref, g_ref, b_ref, o_ref, ob_ref, buf_ref, sem):
    bm = buf_ref.shape[1]

    def issue(g, c):
        for u in range(DMA_UNROLL // 2):
            t = g * (DMA_UNROLL // 2) + u
            for r in range(2):
                _row_copy(y_ref, buf_ref.at[r], sem, d_ref[0, 0, 2 * t + r], t).start(priority=r)
        return c

    def drain(t, c):
        for r in range(2):
            _row_copy(y_ref, buf_ref.at[r], sem, 0, t).wait()
        return c

    lax.fori_loop(0, bm // (DMA_UNROLL // 2), issue, 0)
    lax.fori_loop(0, bm, drain, 0, unroll=DMA_UNROLL // 2)
    gw = gw_ref[...]
    ff = gw[:, 0:1] * buf_ref[0] + gw[:, 1:2] * buf_ref[1]
    y = _layer_norm(DEEPNORM_ALPHA * h_ref[...] + ff, g_ref[...], b_ref[...])
    o_ref[...] = y
    ob_ref[...] = y.astype(BF16)


def _combine(ys, dest, gw, h, g, b):
    T, D = h.shape
    bm = _tile(T, 256)
    row = lambda w: pl.BlockSpec((bm, w), lambda i: (i, 0))
    vec = pl.BlockSpec((1, D), lambda i: (0, 0))
    return pl.pallas_call(
        _combine_kernel,
        grid=(T // bm,),
        in_specs=[pl.BlockSpec((1, 1, 2 * bm), lambda i: (i, 0, 0), memory_space=pltpu.SMEM),
                  pl.BlockSpec(memory_space=pl.ANY), row(LANES), row(D), vec, vec],
        out_specs=[row(D), row(D)],
        out_shape=[jax.ShapeDtypeStruct((T, D), F32), jax.ShapeDtypeStruct((T, D), BF16)],
        scratch_shapes=[pltpu.VMEM((2, bm, D), F32), pltpu.SemaphoreType.DMA((1,))],
        compiler_params=_cparams(("arbitrary",)),
        name="moe_combine",
    )(dest.reshape(T // bm, 1, 2 * bm), ys, gw, h, g.reshape(1, D), b.reshape(1, D))


def _swiglu_accumulate(x_ref, w1, w3, w2, acc_ref):
    hm = x_ref.shape[0] // 2
    rows = [slice(0, hm), slice(hm, 2 * hm)]
    ac = [(_dot(x_ref[r, :], w1), _dot(x_ref[r, :], w3)) for r in rows]
    hh = [(a * _sigmoid(a) * c).astype(BF16) for a, c in ac]
    for r, t in zip(rows, hh):
        acc_ref[r, :] += _dot(t, w2)


def _ffn_kernel(x_ref, w1_ref, w3_ref, w2_ref, h_ref, g_ref, b_ref, o_ref, ob_ref, acc_ref):
    j = pl.program_id(1)

    @pl.when(j == 0)
    def _():
        acc_ref[...] = jnp.zeros_like(acc_ref)

    _swiglu_accumulate(x_ref, w1_ref[0], w3_ref[0], w2_ref[0], acc_ref)

    @pl.when(j == pl.num_programs(1) - 1)
    def _():
        y = _layer_norm(DEEPNORM_ALPHA * h_ref[...] + acc_ref[...], g_ref[...], b_ref[...])
        o_ref[...] = y
        ob_ref[...] = y.astype(BF16)


def _ffn(xb, weights, layer, h, g, b):
    w1, w3, w2 = weights
    M, D = xb.shape
    FF = w1.shape[2]
    bm = _tile(M, 512)
    bf = _tile(FF, 512, LANES)
    row = lambda w: pl.BlockSpec((bm, w), lambda i, j: (i, 0))
    vec = pl.BlockSpec((1, D), lambda i, j: (0, 0))
    return pl.pallas_call(
        _ffn_kernel,
        grid=(M // bm, FF // bf),
        in_specs=[row(D),
                  pl.BlockSpec((1, D, bf), lambda i, j: (layer, 0, j)),
                  pl.BlockSpec((1, D, bf), lambda i, j: (layer, 0, j)),
                  pl.BlockSpec((1, bf, D), lambda i, j: (layer, j, 0)),
                  row(D), vec, vec],
        out_specs=[row(D), row(D)],
        out_shape=[jax.ShapeDtypeStruct((M, D), F32), jax.ShapeDtypeStruct((M, D), BF16)],
        scratch_shapes=[pltpu.VMEM((bm, D), F32)],
        compiler_params=_cparams(("parallel", "arbitrary")),
        name="ffn",
    )(xb, w1, w3, w2, h, g.reshape(1, D), b.reshape(1, D))


RWKV_LORA_W = 512
RWKV_SUM_W = 256


def _shift_lerp(ref, c0, width, r0, p0, notfirst, mu, row0):
    cur = ref[0, pl.ds(r0, CHUNK), c0:c0 + width].astype(F32)
    prev = ref[0, pl.ds(p0, BF16_ROWS), c0:c0 + width][BF16_ROWS - 1:BF16_ROWS, :].astype(F32)
    zprev = jnp.where(row0, prev * notfirst, pltpu.roll(cur, 1, 0))
    return cur + mu * (zprev - cur)


def _split3(x):
    hi = x.astype(BF16)
    r1 = x - hi.astype(F32)
    mid = r1.astype(BF16)
    lo = (r1 - mid.astype(F32)).astype(BF16)
    return hi, mid, lo


def _rwkv_kernel(z_ref, vec_ref, mul_ref, w2_ref, a2_ref, g2_ref, o_ref, s_ref):
    C, N = CHUNK, RWKV_HEAD_DIM
    W = o_ref.shape[2]
    H = W // N
    LW = RWKV_LORA_W
    nchunk = z_ref.shape[1] // C
    s_ref[...] = jnp.zeros_like(s_ref)

    ri = _iota((C, C), 0)
    ci = _iota((C, C), 1)
    strict = ri > ci
    incl = ri >= ci
    tri = incl.astype(BF16)
    eye = (ri == ci).astype(F32)
    SW = RWKV_SUM_W
    bd = (_iota((SW, SW), 0) // N == _iota((SW, SW), 1) // N).astype(BF16)
    row0_w = _iota((C, W), 0) == 0
    row0_l = _iota((C, LW), 0) == 0

    def head_sum(x):
        xb = x.astype(BF16)
        return jnp.concatenate([_dot(xb[:, g:g + SW], bd) for g in range(0, W, SW)], axis=1)

    w0 = vec_ref[0:1, :]
    a0 = vec_ref[1:2, :]
    k_k = vec_ref[2:3, :]
    k_a = vec_ref[3:4, :]
    r_k = vec_ref[4:5, :]
    gn_g = vec_ref[5:6, :]
    gn_b = vec_ref[6:7, :]
    mu_r = vec_ref[8:9, :]
    mu_k = vec_ref[9:10, :]
    mu_v = vec_ref[10:11, :]
    mu_l = mul_ref[0:1, :]

    def body(c, carry):
        r0 = pl.multiple_of(c * C, C)
        p0 = pl.multiple_of(jnp.maximum(r0 - BF16_ROWS, 0), BF16_ROWS)
        notfirst = jnp.where(c > 0, 1.0, 0.0).astype(F32)
        lo = _shift_lerp(z_ref, 0, LW, r0, p0, notfirst, mu_l, row0_l)
        r = _shift_lerp(z_ref, LW, W, r0, p0, notfirst, mu_r, row0_w)
        k = _shift_lerp(z_ref, LW + W, W, r0, p0, notfirst, mu_k, row0_w)
        v = _shift_lerp(z_ref, LW + 2 * W, W, r0, p0, notfirst, mu_v, row0_w)

        wd = lo[:, 0:128]
        ad = lo[:, 128:256]
        gd = lo[:, 256:512]
        w_log = -_softplus(-(w0 + _dot(jnp.tanh(wd).astype(BF16), w2_ref[...]))) - 0.5
        lw = -jnp.exp(w_log)
        a = _sigmoid(a0 + _dot(ad.astype(BF16), a2_ref[...]))
        g = _dot(_sigmoid(gd).astype(BF16), g2_ref[...])

        kk = k * k_k
        kk = kk / jnp.maximum(jnp.sqrt(head_sum(kk * kk)), 1e-12)
        kmod = k * (1.0 + (a - 1.0) * k_a)
        b = kk * a

        cum = sum(_dot(tri, part) for part in _split3(lw))
        cum_c = cum[C - 1:C, :]
        e_inv = jnp.exp(-cum)
        e_fin = jnp.exp(cum_c - cum)
        kkt = kk * jnp.exp(cum - lw)
        rt = r * jnp.exp(cum)
        bt = b * e_inv
        kt = kmod * e_inv
        nbh = -b * e_fin
        kh = kmod * e_fin
        pc = jnp.exp(cum_c)

        hs = range(H)
        sl = [slice(h * N, (h + 1) * N) for h in hs]
        cat = lambda top, bot, h: jnp.concatenate([top[:, sl[h]], bot[:, sl[h]]], axis=0).astype(BF16)
        lhs = [cat(kkt, rt, h) for h in hs]
        rhs = [cat(bt, kt, h) for h in hs]
        vb = [v[:, sl[h]].astype(BF16) for h in hs]
        s_old = [s_ref[h] for h in hs]
        mall = [_dot_nt(lhs[h], rhs[h]) for h in hs]
        gm = [_dot_nt(lhs[h], s_old[h].astype(BF16)) for h in hs]
        mb = [jnp.where(incl, mall[h][C:, :C], 0.0).astype(BF16) for h in hs]
        akmk = [jnp.concatenate([jnp.where(strict, mall[h][:C, C:], 0.0),
                                 jnp.where(incl, mall[h][C:, C:], 0.0)], axis=0).astype(BF16)
                for h in hs]
        av = [_dot(akmk[h], vb[h]) for h in hs]
        square = lambda xs: [_dot(xs[h], xs[h]).astype(BF16) for h in hs]
        x0 = [jnp.where(strict, -mall[h][:C, :C], 0.0) for h in hs]
        t = [eye + x0[h] for h in hs]
        xp = square([x0[h].astype(BF16) for h in hs])
        for level in range(5):
            xn = square(xp) if level < 4 else None
            t = [t[h] + _dot(t[h].astype(BF16), xp[h]) for h in hs]
            xp = xn
        u = [_dot(t[h].astype(BF16), (gm[h][:C] + av[h][:C]).astype(BF16)) for h in hs]
        o = [gm[h][C:] + av[h][C:] - _dot(mb[h], u[h].astype(BF16)) for h in hs]
        upd = [_dot_tn(jnp.concatenate([u[h].astype(BF16), vb[h]], axis=0), cat(nbh, kh, h))
               for h in hs]
        for h in hs:
            s_ref[h] = s_old[h] * pc[:, sl[h]] + upd[h]

        o = jnp.concatenate(o, axis=1)
        d = o - head_sum(o) * (1.0 / N)
        var = head_sum(d * d) * (1.0 / N)
        on = d * lax.rsqrt(var + RWKV_GN_EPS) * gn_g + gn_b
        bonus = head_sum(r * kmod * r_k) * v
        o_ref[0, pl.ds(r0, C), :] = ((on + bonus) * g).astype(o_ref.dtype)
        return carry

    lax.fori_loop(0, nchunk, body, 0)


def _rwkv(z, vecs, mul, w2, a2, g2, B, LP):
    W = vecs.shape[1]
    ZW = z.shape[2]
    whole = lambda shape: pl.BlockSpec(shape, lambda b: (0,) * len(shape))
    return pl.pallas_call(
        _rwkv_kernel,
        grid=(B,),
        in_specs=[pl.BlockSpec((1, LP, ZW), lambda b: (b, 0, 0)),
                  whole((16, W)), whole((8, RWKV_LORA_W)), whole((LANES, W)), whole((LANES, W)),
                  whole((RWKV_GATE_RANK, W))],
        out_specs=pl.BlockSpec((1, LP, W), lambda b: (b, 0, 0)),
        out_shape=jax.ShapeDtypeStruct((B, LP, W), BF16),
        scratch_shapes=[pltpu.VMEM((W // RWKV_HEAD_DIM, RWKV_HEAD_DIM, RWKV_HEAD_DIM), F32)],
        compiler_params=_cparams(("parallel",)),
        name="rwkv7",
    )(z, vecs, mul, w2, a2, g2)


def _gate_prep_kernel(zs_ref, bias_ref, ct_ref, gr_ref, gc_ref):
    LP = zs_ref.shape[1]
    nkb = ct_ref.shape[2]
    TB, C = ATT_BLOCK, CHUNK
    ct_ref[...] = jnp.zeros_like(ct_ref)
    gr_ref[...] = jnp.zeros_like(gr_ref)
    sel = (_iota((16, LANES), 0) == _iota((16, LANES), 1)).astype(F32)
    sel_m = (_iota((SUBLANES, LANES), 1)
             == _iota((SUBLANES, LANES), 0) + MLSTM_GATE_LANE_I).astype(F32)
    carry = jnp.zeros((16, 1), F32)
    for kb in range(nkb):
        n = min(TB, LP - kb * TB)
        gz = zs_ref[0, kb * TB:kb * TB + n, :] + bias_ref[...]
        lf = _log_sigmoid(gz)
        ri = _iota((n, n), 0)
        ci = _iota((n, n), 1)
        cs = _dot(_dot_nt(sel, lf, HI), (ri <= ci).astype(F32), HI) + carry
        carry = cs[:, n - 1:n]
        for hp in range(ct_ref.shape[1]):
            ct_ref[0, hp, kb, 0:2, 0:n] = cs[2 * hp:2 * hp + 2, :]
        same = ri // C == ci // C
        x = jnp.where(_iota((n, LANES), 1) < MLSTM_GATE_LANE_F, gz, lf)
        xt = _dot_nt(sel_m, x, HI)
        bt = _dot(xt, ((ri <= ci) & same).astype(F32), HI)
        rows = jnp.where(_iota((SUBLANES, n), 0) < MLSTM_HEADS, xt, bt)
        for q in range(n // C):
            gr_ref[0, kb * (TB // C) + q, :, 0:C] = rows[:, q * C:(q + 1) * C]
        bc = _dot(((ri >= ci) & same).astype(F32), x, HI)
        gc_ref[0, kb * TB:kb * TB + n, :] = jnp.where(
            _iota((n, LANES), 1) < MLSTM_GATE_LANE_F, x, bc)


def _gate_prep(zs, bias, B, LP, n_pairs):
    nkb = pl.cdiv(LP, ATT_BLOCK)
    nchunk = LP // CHUNK
    return pl.pallas_call(
        _gate_prep_kernel,
        grid=(B,),
        in_specs=[pl.BlockSpec((1, LP, LANES), lambda b: (b, 0, 0)),
                  pl.BlockSpec((1, LANES), lambda b: (0, 0))],
        out_specs=[pl.BlockSpec((1, n_pairs, nkb, SUBLANES, ATT_BLOCK), lambda b: (b, 0, 0, 0, 0)),
                   pl.BlockSpec((1, nchunk, SUBLANES, LANES), lambda b: (b, 0, 0, 0)),
                   pl.BlockSpec((1, LP, LANES), lambda b: (b, 0, 0))],
        out_shape=[jax.ShapeDtypeStruct((B, n_pairs, nkb, SUBLANES, ATT_BLOCK), F32),
                   jax.ShapeDtypeStruct((B, nchunk, SUBLANES, LANES), F32),
                   jax.ShapeDtypeStruct((B, LP, LANES), F32)],
        compiler_params=_cparams(("parallel",)),
        name="gate_prep",
    )(zs, bias)


def _fox_kernel(q_ref, k_ref, v_ref, ct_ref, o_ref):
    TB, N = ATT_BLOCK, FOX_HEAD_DIM
    LP = k_ref.shape[1]
    n_full = LP // TB
    tail = LP - n_full * TB
    qi = pl.program_id(2)

    def attend(rows):
        halves = 2 if rows == TB else 1
        HQ = rows // halves
        units = [(h, s) for h in range(2) for s in range(halves)]
        nu = len(units)
        lane_q = _iota((HQ, LANES), 1)
        scale = N ** -0.5
        qu = [jnp.where((lane_q < N) == (h == 0), q_ref[0, s * HQ:(s + 1) * HQ, :] * scale, 0.0)
              for h, s in units]

        def block(k0, n, cks, masks, state):
            kblk = k_ref[0, pl.ds(k0, n), :]
            vblk = v_ref[0, pl.ds(k0, n), :]
            lane_v = _iota((n, LANES), 1)
            vaug = [jnp.where((lane_v < N) == (h == 0), vblk, 1.0) for h in range(2)]
            sc = [_dot_nt(qu[i], kblk) - cks[h] for i, (h, s) in enumerate(units)]
            if masks is not None:
                sc = [jnp.where(masks[s], sc[i], -jnp.inf) for i, (h, s) in enumerate(units)]
            m_new = [jnp.maximum(state[i][0], jnp.max(sc[i], axis=-1, keepdims=True))
                     for i in range(nu)]
            p = [jnp.exp(sc[i] - m_new[i]).astype(BF16) for i in range(nu)]
            pv = [_dot(p[i], vaug[h]) for i, (h, s) in enumerate(units)]
            return tuple((m_new[i], jnp.exp(state[i][0] - m_new[i]) * state[i][1] + pv[i])
                         for i in range(nu))

        ck = lambda kb, h: ct_ref[0, 0, kb, h:h + 1, :]

        def pair_body(kp, state):
            cks = [jnp.concatenate([ck(2 * kp, h), ck(2 * kp + 1, h)], axis=1) for h in range(2)]
            return block(pl.multiple_of(kp * 2 * TB, 2 * TB), 2 * TB, cks, None, state)

        def single_body(_, state):
            kb = qi - 1
            return block(pl.multiple_of(kb * TB, TB), TB, [ck(kb, h) for h in range(2)], None,
                         state)

        state = tuple((jnp.full((HQ, 1), NEG_BIG, F32), jnp.zeros((HQ, LANES), F32))
                      for _ in units)
        state = lax.fori_loop(0, lax.shift_right_logical(qi, 1), pair_body, state)
        state = lax.fori_loop(0, qi & 1, single_body, state)
        cks = [ck(qi, h)[:, 0:rows] for h in range(2)]
        masks = [_iota((HQ, rows), 0) + s * HQ >= _iota((HQ, rows), 1) for s in range(halves)]
        final = block(pl.multiple_of(qi * TB, TB), rows, cks, masks, state)
        for s in range(halves):
            acc0, acc1 = final[s][1], final[halves + s][1]
            out = jnp.where(lane_q < N, acc0 / acc0[:, N:N + 1], acc1 / acc1[:, 0:1])
            o_ref[0, s * HQ:(s + 1) * HQ, :] = out.astype(o_ref.dtype)

    if tail == 0:
        attend(TB)
    else:
        pl.when(qi < n_full)(lambda: attend(TB))
        pl.when(qi == n_full)(lambda: attend(tail))


def _fox(zq, ct, B, LP, width):
    n_pairs = width // LANES
    nq = pl.cdiv(LP, ATT_BLOCK)
    nkb = ct.shape[2]
    kv = lambda off: pl.BlockSpec((1, LP, LANES), lambda b, p, i: (b, 0, off + p))
    return pl.pallas_call(
        _fox_kernel,
        grid=(B, n_pairs, nq),
        in_specs=[pl.BlockSpec((1, ATT_BLOCK, LANES), lambda b, p, i: (b, i, p)),
                  kv(n_pairs), kv(2 * n_pairs),
                  pl.BlockSpec((1, 1, nkb, SUBLANES, ATT_BLOCK), lambda b, p, i: (b, p, 0, 0, 0))],
        out_specs=pl.BlockSpec((1, ATT_BLOCK, LANES), lambda b, p, i: (b, i, p)),
        out_shape=jax.ShapeDtypeStruct((B, LP, width), BF16),
        compiler_params=_cparams(("parallel", "parallel", "arbitrary")),
        name="fox_attention",
    )(zq, zq, zq, ct)


MLSTM_GATE_LANE_I = 12
MLSTM_GATE_LANE_F = 16


def _conv_silu(ref, r0, p0, notfirst, w_ref):
    cur = ref[0, pl.ds(r0, CHUNK), :].astype(F32)
    prev = ref[0, pl.ds(p0, BF16_ROWS), :][BF16_ROWS - SUBLANES:, :].astype(F32) * notfirst
    row8 = _iota(prev.shape, 0)
    y = cur * w_ref[MLSTM_CONV - 1:MLSTM_CONV, :]
    for s in range(1, MLSTM_CONV):
        rolled = pltpu.roll(cur, s, 0)
        top = jnp.where(row8 < s, pltpu.roll(prev, s, 0), rolled[0:SUBLANES])
        shifted = jnp.concatenate([top, rolled[SUBLANES:]], axis=0)
        y = y + shifted * w_ref[MLSTM_CONV - 1 - s:MLSTM_CONV - s, :]
    return y * _sigmoid(y)


def _mlstm_kernel(q_ref, k_ref, v_ref, og_ref, gr_ref, gc_ref, cwq_ref, cwk_ref,
                  o_ref, c_ref, m_ref):
    C, H = CHUNK, MLSTM_HEADS
    Dh = q_ref.shape[2] // H
    nchunk = q_ref.shape[1] // C
    c_ref[...] = jnp.zeros_like(c_ref)
    m_ref[...] = jnp.zeros_like(m_ref)

    incl = _iota((C, C), 0) >= _iota((C, C), 1)
    ones_col = (_iota((C, LANES), 1) == 0).astype(BF16)

    def body(c, carry):
        r0 = pl.multiple_of(c * C, C)
        p0 = pl.multiple_of(jnp.maximum(r0 - BF16_ROWS, 0), BF16_ROWS)
        notfirst = jnp.where(c > 0, 1.0, 0.0).astype(F32)
        q = _conv_silu(q_ref, r0, p0, notfirst, cwq_ref)
        k = _conv_silu(k_ref, r0, p0, notfirst, cwk_ref) * (Dh ** -0.5)
        v = v_ref[0, pl.ds(r0, C), :]
        og = og_ref[0, pl.ds(r0, C), :].astype(F32)
        g_rows = gr_ref[0, c]
        g_cols = gc_ref[0, pl.ds(r0, C), :]

        hs = range(H)
        sl = [slice(h * Dh, (h + 1) * Dh) for h in hs]
        m_all = m_ref[...]
        m_old = [m_all[h:h + 1, 0:1] for h in hs]
        c_old = [c_ref[h] for h in hs]
        li_r = [g_rows[h:h + 1, 0:C] for h in hs]
        b_r = [g_rows[H + h:H + h + 1, 0:C] for h in hs]
        li_c = [g_cols[:, MLSTM_GATE_LANE_I + h:MLSTM_GATE_LANE_I + h + 1] for h in hs]
        b_c = [g_cols[:, MLSTM_GATE_LANE_F + h:MLSTM_GATE_LANE_F + h + 1] for h in hs]
        qh = [q[:, sl[h]].astype(BF16) for h in hs]
        kh = [k[:, sl[h]] for h in hs]
        vaug = [jnp.concatenate([v[:, sl[h]], ones_col], axis=1) for h in hs]
        qk = [_dot_nt(qh[h], kh[h].astype(BF16)) for h in hs]
        qc = [_dot(qh[h], c_old[h].astype(BF16)) for h in hs]
        log_d = [jnp.where(incl, b_c[h] - b_r[h] + li_r[h], -jnp.inf) for h in hs]
        log_inter = [b_c[h] + m_old[h] for h in hs]
        m_q = [jnp.maximum(log_inter[h], jnp.max(log_d[h], axis=-1, keepdims=True)) for h in hs]
        s = [(qk[h] * jnp.exp(log_d[h] - m_q[h])).astype(BF16) for h in hs]
        b_end = [b_c[h][C - 1:C, :] for h in hs]
        m_new = [jnp.maximum(b_end[h] + m_old[h],
                             jnp.max(b_end[h] - b_r[h] + li_r[h], axis=-1, keepdims=True))
                 for h in hs]
        gk = [jnp.exp(b_end[h] - b_c[h] + li_c[h] - m_new[h]) for h in hs]
        num = [jnp.exp(log_inter[h] - m_q[h]) * qc[h] + _dot(s[h], vaug[h]) for h in hs]
        kv = [_dot_tn((gk[h] * kh[h]).astype(BF16), vaug[h]) for h in hs]
        outs = [_sigmoid(og[:, sl[h]]) * num[h][:, :Dh]
                / jnp.maximum(jnp.abs(num[h][:, Dh:Dh + 1]), jnp.exp(-m_q[h])) for h in hs]
        o_ref[0, pl.ds(r0, C), :] = jnp.concatenate(outs, axis=1).astype(o_ref.dtype)
        for h in hs:
            c_ref[h] = jnp.exp(b_end[h] + m_old[h] - m_new[h]) * c_old[h] + kv[h]
        m_ref[0:H, :] = jnp.concatenate([jnp.broadcast_to(m_new[h], (1, LANES)) for h in hs],
                                        axis=0)
        return carry

    lax.fori_loop(0, nchunk, body, 0)


def _mlstm(zm, g_rows, g_cols, cwq, cwk, B, LP):
    W = zm.shape[2] // 4
    Dh = W // MLSTM_HEADS
    seq = lambda j: pl.BlockSpec((1, LP, W), lambda b: (b, 0, j))
    cw = pl.BlockSpec((SUBLANES, W), lambda b: (0, 0))
    return pl.pallas_call(
        _mlstm_kernel,
        grid=(B,),
        in_specs=[seq(0), seq(1), seq(2), seq(3),
                  pl.BlockSpec((1, LP // CHUNK, SUBLANES, LANES), lambda b: (b, 0, 0, 0)),
                  pl.BlockSpec((1, LP, LANES), lambda b: (b, 0, 0)), cw, cw],
        out_specs=pl.BlockSpec((1, LP, W), lambda b: (b, 0, 0)),
        out_shape=jax.ShapeDtypeStruct((B, LP, W), BF16),
        scratch_shapes=[pltpu.VMEM((MLSTM_HEADS, Dh, 2 * Dh), F32),
                        pltpu.VMEM((SUBLANES, LANES), F32)],
        compiler_params=_cparams(("parallel",)),
        name="mlstm",
    )(zm, zm, zm, zm, g_rows, g_cols, cwq, cwk)


def _pad_rows(w, n):
    return jnp.pad(w, ((0, n - w.shape[0]), (0, 0)))


def kernel(x, meta_tokens, ln_emb_g, ln_emb_b, w_in, rwkv_mu, rwkv_w0, rwkv_w2, rwkv_a0, rwkv_a2, rwkv_g2, rwkv_k_k, rwkv_k_a, rwkv_r_k, rwkv_gn_g, rwkv_gn_b, fox_b_f, mlstm_conv_w, mlstm_b_i, mlstm_b_f, proj_rwkv, proj_fox, proj_mlstm, w_out, ln1_g, ln1_b, ffn_w1, ffn_w3, ffn_w2, router_w, moe_w1, moe_w3, moe_w2, ln2_g, ln2_b):
    B, S, D = x.shape
    L = N_META + S
    LP = -(-L // CHUNK) * CHUNK
    T = B * LP
    RW = proj_rwkv.shape[1]
    FW = proj_fox.shape[1]
    MW = proj_mlstm.shape[1]
    n_fox_heads = FW // FOX_HEAD_DIM
    dr, ar, gr = RWKV_DECAY_RANK, RWKV_AAA_RANK, RWKV_GATE_RANK
    c1 = 3 * RW + dr + ar + gr

    meta = jnp.broadcast_to(meta_tokens[None].astype(x.dtype), (B, N_META, D))
    hin = jnp.concatenate([meta, x, jnp.zeros((B, LP - L, D), x.dtype)], axis=1).reshape(T, D)
    h, hb = _ln(hin, ln_emb_g, ln_emb_b)

    ffn_b = tuple(w.astype(BF16) for w in (ffn_w1, ffn_w3, ffn_w2))
    moe_b = tuple(w.astype(BF16) for w in (moe_w1, moe_w3, moe_w2))

    for l in range(DEPTH):
        w_r, w_f, w_m, w_s, w_g = _prep_w_in(w_in, l, RW, FW, MW, dr, ar, gr)

        z_r = _mm(hb, w_r, BF16, 1408, "in_rwkv").reshape(B, LP, -1)
        z_f = _mm(hb, w_f, BF16, 1152, "in_fox").reshape(B, LP, -1)
        z_m = _mm(hb, w_m, BF16, 1024, "in_mlstm").reshape(B, LP, -1)
        z_s = _mm(hb, w_s, F32, LANES, "in_gates").reshape(B, LP, LANES)

        mu = rwkv_mu[l]
        vecs = jnp.zeros((16, RW), F32)
        for i, p in enumerate([rwkv_w0[l], rwkv_a0[l], rwkv_k_k[l], rwkv_k_a[l],
                               rwkv_r_k[l].reshape(RW), rwkv_gn_g[l], rwkv_gn_b[l]]):
            vecs = vecs.at[i].set(p)
        vecs = vecs.at[8].set(mu[:RW]).at[9].set(mu[RW:2 * RW]).at[10].set(mu[2 * RW:3 * RW])
        mul = jnp.zeros((8, RWKV_LORA_W), F32)
        mul = mul.at[0, 0:dr].set(mu[3 * RW:3 * RW + dr])
        mul = mul.at[0, LANES:LANES + ar].set(mu[3 * RW + dr:3 * RW + dr + ar])
        mul = mul.at[0, 2 * LANES:2 * LANES + gr].set(mu[3 * RW + dr + ar:c1])
        y_r = _rwkv(z_r, vecs, mul, _pad_rows(rwkv_w2[l], LANES).astype(BF16),
                    _pad_rows(rwkv_a2[l], LANES).astype(BF16), rwkv_g2[l].astype(BF16), B, LP)

        bias = jnp.zeros((1, LANES), F32)
        bias = bias.at[0, 0:n_fox_heads].set(fox_b_f[l])
        bias = bias.at[0, MLSTM_GATE_LANE_I:MLSTM_GATE_LANE_I + MLSTM_HEADS].set(mlstm_b_i[l])
        bias = bias.at[0, MLSTM_GATE_LANE_F:MLSTM_GATE_LANE_F + MLSTM_HEADS].set(mlstm_b_f[l])

        ct, g_rows, g_cols = _gate_prep(z_s, bias, B, LP, FW // LANES)
        y_f = _fox(z_f, ct, B, LP, FW)

        cw = _pad_rows(mlstm_conv_w[l], SUBLANES)
        y_m = _mlstm(z_m, g_rows, g_cols, cw[:, :MW], cw[:, MW:], B, LP)

        pre = _merge(hb, w_g, y_r.reshape(T, RW), y_f.reshape(T, FW), y_m.reshape(T, MW),
                     proj_rwkv[l].astype(BF16), proj_fox[l].astype(BF16),
                     proj_mlstm[l].astype(BF16))
        h, hb = _mm_res_ln(pre, w_out[l].astype(BF16), h, ln1_g[l], ln1_b[l])

        i = l // 2
        if l % 2 == 0:
            h, hb = _ffn(hb, ffn_b, i, h, ln2_g[l], ln2_b[l])
        else:
            tm = _tile(T, MOE_TILE)
            n_tiles = 2 * T // tm + N_EXPERTS
            gw, ids = _router(h, router_w[i])
            row_token, dest, tile_expert, n_used = _route(ids, tm, n_tiles)
            xs = _gather_rows(h, row_token)
            ys = _moe_ffn(xs, moe_b, i, tile_expert, n_used, tm)
            h, hb = _combine(ys, dest, gw, h, ln2_g[l], ln2_b[l])

    return h.reshape(B, LP, D)[:, N_META:L]
```

```python
import functools
import math

import jax
import jax.numpy as jnp
from jax import lax
from jax.experimental import pallas as pl
from jax.experimental.pallas import tpu as pltpu

F32 = jnp.float32
BF16 = jnp.bfloat16
HI = lax.Precision.HIGHEST

N_META = 16
DEPTH = 4
RWKV_HEAD_DIM = 64
RWKV_DECAY_RANK = 96
RWKV_AAA_RANK = 96
RWKV_GATE_RANK = 256
RWKV_GN_EPS = 64e-5
FOX_HEAD_DIM = 64
MLSTM_HEADS = 4
MLSTM_CONV = 4
N_EXPERTS = 8
LN_EPS = 1e-5
DEEPNORM_ALPHA = (2 * DEPTH) ** 0.25

CHUNK = 64
LANES = 128
SUBLANES = 8
BF16_ROWS = 16
ATT_BLOCK = 512
MOE_TILE = 1024
DMA_UNROLL = 8
VMEM_LIMIT = 56 * 1024 * 1024
NEG_BIG = -1e30


def _cparams(sem):
    return pltpu.CompilerParams(dimension_semantics=sem, vmem_limit_bytes=VMEM_LIMIT)


def _tile(n, pref, mult=SUBLANES):
    best = None
    for d in range(mult, min(n, pref) + 1, mult):
        if n % d == 0:
            best = d
    return best if best is not None else n


def _dot(a, b, precision=None):
    return jnp.dot(a, b, preferred_element_type=F32, precision=precision)


def _dot_nt(a, b, precision=None):
    return lax.dot_general(a, b, (((1,), (1,)), ((), ())), preferred_element_type=F32,
                           precision=precision)


def _dot_tn(a, b, precision=None):
    return lax.dot_general(a, b, (((0,), (0,)), ((), ())), preferred_element_type=F32,
                           precision=precision)


def _softplus(x):
    return jnp.maximum(x, 0.0) + jnp.log(1.0 + jnp.exp(-jnp.abs(x)))


def _log_sigmoid(x):
    return jnp.minimum(x, 0.0) - jnp.log(1.0 + jnp.exp(-jnp.abs(x)))


def _sigmoid(x):
    return 1.0 / (1.0 + jnp.exp(-x))


def _iota(shape, dim):
    return lax.broadcasted_iota(jnp.int32, shape, dim)


def _layer_norm(t, g, b):
    mu = jnp.mean(t, axis=-1, keepdims=True)
    d = t - mu
    var = jnp.mean(d * d, axis=-1, keepdims=True)
    return d * lax.rsqrt(var + LN_EPS) * g + b


def _ln_kernel(x_ref, g_ref, b_ref, o_ref, ob_ref):
    y = _layer_norm(x_ref[...], g_ref[...], b_ref[...])
    o_ref[...] = y
    ob_ref[...] = y.astype(BF16)


def _ln(x, g, b):
    T, D = x.shape
    bm = _tile(T, 512)
    return pl.pallas_call(
        _ln_kernel,
        grid=(T // bm,),
        in_specs=[pl.BlockSpec((bm, D), lambda i: (i, 0)),
                  pl.BlockSpec((1, D), lambda i: (0, 0)),
                  pl.BlockSpec((1, D), lambda i: (0, 0))],
        out_specs=[pl.BlockSpec((bm, D), lambda i: (i, 0)),
                   pl.BlockSpec((bm, D), lambda i: (i, 0))],
        out_shape=[jax.ShapeDtypeStruct((T, D), F32), jax.ShapeDtypeStruct((T, D), BF16)],
        compiler_params=_cparams(("parallel",)),
        name="ln_embed",
    )(x, g.reshape(1, D), b.reshape(1, D))


def _prep_w_in_kernel(w_ref, or_ref, of_ref, om_ref, os_ref, og_ref, *, rw, fw, mw, dr, ar):
    x = w_ref[0]
    c1 = or_ref.shape[1] - 2 * LANES + dr + ar
    c2 = c1 + 3 * fw + fw // FOX_HEAD_DIM
    c3 = c2 + 4 * mw + 2 * MLSTM_HEADS
    zeros = lambda n: jnp.zeros((x.shape[0], n), F32)
    cast = lambda parts: jnp.concatenate(parts, axis=1).astype(BF16)
    or_ref[...] = cast([x[:, 3 * rw:3 * rw + dr], zeros(LANES - dr),
                        x[:, 3 * rw + dr:3 * rw + dr + ar], zeros(LANES - ar),
                        x[:, 3 * rw + dr + ar:c1], x[:, 0:3 * rw]])
    of_ref[...] = x[:, c1:c1 + 3 * fw].astype(BF16)
    om_ref[...] = x[:, c2:c2 + 4 * mw].astype(BF16)
    n_small = c2 - (c1 + 3 * fw) + c3 - (c2 + 4 * mw)
    os_ref[...] = cast([x[:, c1 + 3 * fw:c2], x[:, c2 + 4 * mw:c3], zeros(LANES - n_small)])
    og_ref[...] = x[:, c3:].astype(BF16)


def _prep_w_in(w_in, l, rw, fw, mw, dr, ar, gr):
    _, D, n_in = w_in.shape
    rows = _tile(D, 128, BF16_ROWS)
    widths = [3 * rw + 2 * LANES + gr, 3 * fw, 4 * mw, LANES, 3 * D]
    return pl.pallas_call(
        functools.partial(_prep_w_in_kernel, rw=rw, fw=fw, mw=mw, dr=dr, ar=ar),
        grid=(D // rows,),
        in_specs=[pl.BlockSpec((1, rows, n_in), lambda i: (l, i, 0))],
        out_specs=[pl.BlockSpec((rows, w), lambda i: (i, 0)) for w in widths],
        out_shape=[jax.ShapeDtypeStruct((D, w), BF16) for w in widths],
        compiler_params=_cparams(("parallel",)),
        name="prep_w_in",
    )(w_in)


def _mm_kernel(a_ref, b_ref, o_ref):
    o_ref[...] = _dot(a_ref[...], b_ref[...]).astype(o_ref.dtype)


def _mm(a, b, out_dtype, bn_pref, name):
    M, K = a.shape
    N = b.shape[1]
    bm = _tile(M, 1024)
    bn = _tile(N, bn_pref, LANES)
    return pl.pallas_call(
        _mm_kernel,
        grid=(M // bm, N // bn),
        in_specs=[pl.BlockSpec((bm, K), lambda i, j: (i, 0)),
                  pl.BlockSpec((K, bn), lambda i, j: (0, j))],
        out_specs=pl.BlockSpec((bm, bn), lambda i, j: (i, j)),
        out_shape=jax.ShapeDtypeStruct((M, N), out_dtype),
        compiler_params=_cparams(("parallel", "arbitrary")),
        name=name,
    )(a, b)


def _mm_res_ln_kernel(a_ref, w_ref, h_ref, g_ref, b_ref, o_ref, ob_ref):
    t = DEEPNORM_ALPHA * h_ref[...] + _dot(a_ref[...], w_ref[...])
    y = _layer_norm(t, g_ref[...], b_ref[...])
    o_ref[...] = y
    ob_ref[...] = y.astype(BF16)


def _mm_res_ln(a, w, h, g, b):
    M, K = a.shape
    D = w.shape[1]
    bm = _tile(M, 512)
    return pl.pallas_call(
        _mm_res_ln_kernel,
        grid=(M // bm,),
        in_specs=[pl.BlockSpec((bm, K), lambda i: (i, 0)),
                  pl.BlockSpec((K, D), lambda i: (0, 0)),
                  pl.BlockSpec((bm, D), lambda i: (i, 0)),
                  pl.BlockSpec((1, D), lambda i: (0, 0)),
                  pl.BlockSpec((1, D), lambda i: (0, 0))],
        out_specs=[pl.BlockSpec((bm, D), lambda i: (i, 0)),
                   pl.BlockSpec((bm, D), lambda i: (i, 0))],
        out_shape=[jax.ShapeDtypeStruct((M, D), F32), jax.ShapeDtypeStruct((M, D), BF16)],
        compiler_params=_cparams(("parallel",)),
        name="out_proj_ln",
    )(a, w, h, g.reshape(1, D), b.reshape(1, D))


def _merge_kernel(hb_ref, wgr_ref, wgf_ref, wgm_ref, yr_ref, yf_ref, ym_ref, pr_ref, pf_ref,
                  pm_ref, o_ref):
    hb = hb_ref[...]
    acc = _sigmoid(_dot(hb, wgr_ref[...])) * _dot(yr_ref[...], pr_ref[...])
    acc += _sigmoid(_dot(hb, wgf_ref[...])) * _dot(yf_ref[...], pf_ref[...])
    acc += _sigmoid(_dot(hb, wgm_ref[...])) * _dot(ym_ref[...], pm_ref[...])
    o_ref[...] = acc.astype(o_ref.dtype)


def _merge(hb, wg, yr, yf, ym, pr, pf, pm):
    M, D = hb.shape
    bm = _tile(M, 1024)
    bn = _tile(D, 512, LANES)
    nb = D // bn
    row = lambda w: pl.BlockSpec((bm, w), lambda i, j: (i, 0))
    col = lambda k: pl.BlockSpec((k, bn), lambda i, j: (0, j))
    gate = lambda br: pl.BlockSpec((D, bn), lambda i, j: (0, br * nb + j))
    return pl.pallas_call(
        _merge_kernel,
        grid=(M // bm, nb),
        in_specs=[row(D), gate(0), gate(1), gate(2),
                  row(yr.shape[1]), row(yf.shape[1]), row(ym.shape[1]),
                  col(pr.shape[0]), col(pf.shape[0]), col(pm.shape[0])],
        out_specs=pl.BlockSpec((bm, bn), lambda i, j: (i, j)),
        out_shape=jax.ShapeDtypeStruct((M, D), BF16),
        compiler_params=_cparams(("parallel", "arbitrary")),
        name="merge",
    )(hb, wg, wg, wg, yr, yf, ym, pr, pf, pm)


def _router_kernel(h_ref, w_ref, ow_ref, oi_ref):
    logits = _dot(h_ref[...], w_ref[...], HI)
    lane = _iota(logits.shape, 1)
    logits = jnp.where(lane < N_EXPERTS, logits, -jnp.inf)
    m1 = jnp.max(logits, axis=-1, keepdims=True)
    i1 = jnp.min(jnp.where(logits == m1, lane, LANES), axis=-1, keepdims=True)
    rest = jnp.where(lane == i1, -jnp.inf, logits)
    m2 = jnp.max(rest, axis=-1, keepdims=True)
    i2 = jnp.min(jnp.where(rest == m2, lane, LANES), axis=-1, keepdims=True)
    e2 = jnp.exp(m2 - m1)
    den = 1.0 + e2
    ow_ref[...] = jnp.where(lane == 0, 1.0 / den, jnp.where(lane == 1, e2 / den, 0.0))
    oi_ref[...] = jnp.where(lane == 0, i1, jnp.where(lane == 1, i2, 0))


def _router(h, w):
    T, D = h.shape
    bm = _tile(T, 512)
    wp = jnp.zeros((D, LANES), F32).at[:, :N_EXPERTS].set(w)
    out = pl.BlockSpec((bm, LANES), lambda i: (i, 0))
    return pl.pallas_call(
        _router_kernel,
        grid=(T // bm,),
        in_specs=[pl.BlockSpec((bm, D), lambda i: (i, 0)),
                  pl.BlockSpec((D, LANES), lambda i: (0, 0))],
        out_specs=[out, out],
        out_shape=[jax.ShapeDtypeStruct((T, LANES), F32),
                   jax.ShapeDtypeStruct((T, LANES), jnp.int32)],
        compiler_params=_cparams(("parallel",)),
        name="router",
    )(h, wp)


def _route(ids, tm, n_tiles):
    T = ids.shape[0]
    e = ids[:, :2].reshape(-1)
    onehot = (e[:, None] == jnp.arange(N_EXPERTS, dtype=jnp.int32)[None, :]).astype(jnp.int32)
    pos = jnp.take_along_axis(jnp.cumsum(onehot, axis=0), e[:, None], axis=1)[:, 0] - 1
    counts = jnp.sum(onehot, axis=0)
    padded = (counts + tm - 1) // tm * tm
    ends = jnp.cumsum(padded)
    dest = (ends - padded)[e] + pos
    row_token = jnp.zeros((n_tiles * tm,), jnp.int32).at[dest].set(
        jnp.arange(2 * T, dtype=jnp.int32) // 2)
    tile_start = jnp.arange(n_tiles, dtype=jnp.int32) * tm
    tile_expert = jnp.minimum(
        jnp.sum((ends[None, :] <= tile_start[:, None]).astype(jnp.int32), axis=1), N_EXPERTS - 1)
    n_used = (ends[-1] // tm).astype(jnp.int32).reshape(1)
    return row_token.reshape(n_tiles, 1, tm), dest, tile_expert, n_used


def _row_copy(src_ref, buf_ref, sem, src_row, dst_row):
    return pltpu.make_async_copy(src_ref.at[pl.ds(src_row, 1), :],
                                 buf_ref.at[pl.ds(dst_row, 1), :], sem.at[0])


def _gather_kernel(idx_ref, src_ref, o_ref, buf_ref, sem):
    tm = buf_ref.shape[0]

    def issue(r, c):
        _row_copy(src_ref, buf_ref, sem, idx_ref[0, 0, r], r).start()
        return c

    def drain(r, c):
        _row_copy(src_ref, buf_ref, sem, 0, r).wait()
        return c

    lax.fori_loop(0, tm, issue, 0, unroll=DMA_UNROLL)
    lax.fori_loop(0, tm, drain, 0, unroll=DMA_UNROLL)
    o_ref[...] = buf_ref[...].astype(o_ref.dtype)


def _gather_rows(src, row_token):
    n_tiles, _, tm = row_token.shape
    D = src.shape[1]
    return pl.pallas_call(
        _gather_kernel,
        grid=(n_tiles,),
        in_specs=[pl.BlockSpec((1, 1, tm), lambda i: (i, 0, 0), memory_space=pltpu.SMEM),
                  pl.BlockSpec(memory_space=pl.ANY)],
        out_specs=pl.BlockSpec((tm, D), lambda i: (i, 0)),
        out_shape=jax.ShapeDtypeStruct((n_tiles * tm, D), BF16),
        scratch_shapes=[pltpu.VMEM((tm, D), F32), pltpu.SemaphoreType.DMA((1,))],
        compiler_params=_cparams(("arbitrary",)),
        name="moe_gather",
    )(row_token, src)


def _moe_ffn_kernel(te_ref, nu_ref, x_ref, w1_ref, w3_ref, w2_ref, o_ref):
    j = pl.program_id(1)

    @pl.when(j == 0)
    def _():
        o_ref[...] = jnp.zeros_like(o_ref)

    @pl.when(pl.program_id(0) < nu_ref[0])
    def _():
        _swiglu_accumulate(x_ref, w1_ref[0, 0], w3_ref[0, 0], w2_ref[0, 0], o_ref)


def _moe_ffn(xs, weights, layer, tile_expert, n_used, tm):
    w1, w3, w2 = weights
    NP, D = xs.shape
    FF = w1.shape[3]
    bf = _tile(FF, 512, LANES)
    grid_spec = pltpu.PrefetchScalarGridSpec(
        num_scalar_prefetch=2,
        grid=(NP // tm, FF // bf),
        in_specs=[pl.BlockSpec((tm, D), lambda i, j, te, nu: (i, 0)),
                  pl.BlockSpec((1, 1, D, bf), lambda i, j, te, nu: (layer, te[i], 0, j)),
                  pl.BlockSpec((1, 1, D, bf), lambda i, j, te, nu: (layer, te[i], 0, j)),
                  pl.BlockSpec((1, 1, bf, D), lambda i, j, te, nu: (layer, te[i], j, 0))],
        out_specs=pl.BlockSpec((tm, D), lambda i, j, te, nu: (i, 0)),
    )
    return pl.pallas_call(
        _moe_ffn_kernel,
        grid_spec=grid_spec,
        out_shape=jax.ShapeDtypeStruct((NP, D), F32),
        compiler_params=_cparams(("parallel", "arbitrary")),
        name="moe_ffn",
    )(tile_expert, n_used, xs, w1, w3, w2)


def _combine_kernel(d_ref, y_ref, gw_ref, h_ref, g_ref, b_ref, o_ref, ob_ref, buf_ref, sem):
    bm = buf_ref.shape[1]

    def issue(g, c):
        for u in range(DMA_UNROLL // 2):
            t = g * (DMA_UNROLL // 2) + u
            for r in range(2):
                _row_copy(y_ref, buf_ref.at[r], sem, d_ref[0, 0, 2 * t + r], t).start(priority=r)
        return c

    def drain(t, c):
        for r in range(2):
            _row_copy(y_ref, buf_ref.at[r], sem, 0, t).wait()
        return c

    lax.fori_loop(0, bm // (DMA_UNROLL // 2), issue, 0)
    lax.fori_loop(0, bm, drain, 0, unroll=DMA_UNROLL // 2)
    gw = gw_ref[...]
    ff = gw[:, 0:1] * buf_ref[0] + gw[:, 1:2] * buf_ref[1]
    y = _layer_norm(DEEPNORM_ALPHA * h_ref[...] + ff, g_ref[...], b_ref[...])
    o_ref[...] = y
    ob_ref[...] = y.astype(BF16)


def _combine(ys, dest, gw, h, g, b):
    T, D = h.shape
    bm = _tile(T, 256)
    row = lambda w: pl.BlockSpec((bm, w), lambda i: (i, 0))
    vec = pl.BlockSpec((1, D), lambda i: (0, 0))
    return pl.pallas_call(
        _combine_kernel,
        grid=(T // bm,),
        in_specs=[pl.BlockSpec((1, 1, 2 * bm), lambda i: (i, 0, 0), memory_space=pltpu.SMEM),
                  pl.BlockSpec(memory_space=pl.ANY), row(LANES), row(D), vec, vec],
        out_specs=[row(D), row(D)],
        out_shape=[jax.ShapeDtypeStruct((T, D), F32), jax.ShapeDtypeStruct((T, D), BF16)],
        scratch_shapes=[pltpu.VMEM((2, bm, D), F32), pltpu.SemaphoreType.DMA((1,))],
        compiler_params=_cparams(("arbitrary",)),
        name="moe_combine",
    )(dest.reshape(T // bm, 1, 2 * bm), ys, gw, h, g.reshape(1, D), b.reshape(1, D))


def _swiglu_accumulate(x_ref, w1, w3, w2, acc_ref):
    hm = x_ref.shape[0] // 2
    rows = [slice(0, hm), slice(hm, 2 * hm)]
    ac = [(_dot(x_ref[r, :], w1), _dot(x_ref[r, :], w3)) for r in rows]
    hh = [(a * _sigmoid(a) * c).astype(BF16) for a, c in ac]
    for r, t in zip(rows, hh):
        acc_ref[r, :] += _dot(t, w2)


def _ffn_kernel(x_ref, w1_ref, w3_ref, w2_ref, h_ref, g_ref, b_ref, o_ref, ob_ref, acc_ref):
    j = pl.program_id(1)

    @pl.when(j == 0)
    def _():
        acc_ref[...] = jnp.zeros_like(acc_ref)

    _swiglu_accumulate(x_ref, w1_ref[0], w3_ref[0], w2_ref[0], acc_ref)

    @pl.when(j == pl.num_programs(1) - 1)
    def _():
        y = _layer_norm(DEEPNORM_ALPHA * h_ref[...] + acc_ref[...], g_ref[...], b_ref[...])
        o_ref[...] = y
        ob_ref[...] = y.astype(BF16)


def _ffn(xb, weights, layer, h, g, b):
    w1, w3, w2 = weights
    M, D = xb.shape
    FF = w1.shape[2]
    bm = _tile(M, 512)
    bf = _tile(FF, 512, LANES)
    row = lambda w: pl.BlockSpec((bm, w), lambda i, j: (i, 0))
    vec = pl.BlockSpec((1, D), lambda i, j: (0, 0))
    return pl.pallas_call(
        _ffn_kernel,
        grid=(M // bm, FF // bf),
        in_specs=[row(D),
                  pl.BlockSpec((1, D, bf), lambda i, j: (layer, 0, j)),
                  pl.BlockSpec((1, D, bf), lambda i, j: (layer, 0, j)),
                  pl.BlockSpec((1, bf, D), lambda i, j: (layer, j, 0)),
                  row(D), vec, vec],
        out_specs=[row(D), row(D)],
        out_shape=[jax.ShapeDtypeStruct((M, D), F32), jax.ShapeDtypeStruct((M, D), BF16)],
        scratch_shapes=[pltpu.VMEM((bm, D), F32)],
        compiler_params=_cparams(("parallel", "arbitrary")),
        name="ffn",
    )(xb, w1, w3, w2, h, g.reshape(1, D), b.reshape(1, D))


RWKV_LORA_W = 512
RWKV_SUM_W = 256


def _shift_lerp(ref, c0, width, r0, p0, notfirst, mu, row0):
    cur = ref[0, pl.ds(r0, CHUNK), c0:c0 + width].astype(F32)
    prev = ref[0, pl.ds(p0, BF16_ROWS), c0:c0 + width][BF16_ROWS - 1:BF16_ROWS, :].astype(F32)
    zprev = jnp.where(row0, prev * notfirst, pltpu.roll(cur, 1, 0))
    return cur + mu * (zprev - cur)


def _split3(x):
    hi = x.astype(BF16)
    r1 = x - hi.astype(F32)
    mid = r1.astype(BF16)
    lo = (r1 - mid.astype(F32)).astype(BF16)
    return hi, mid, lo


def _rwkv_kernel(z_ref, vec_ref, mul_ref, w2_ref, a2_ref, g2_ref, o_ref, s_ref):
    C, N = CHUNK, RWKV_HEAD_DIM
    W = o_ref.shape[2]
    H = W // N
    LW = RWKV_LORA_W
    nchunk = z_ref.shape[1] // C
    s_ref[...] = jnp.zeros_like(s_ref)

    ri = _iota((C, C), 0)
    ci = _iota((C, C), 1)
    strict = ri > ci
    incl = ri >= ci
    tri = incl.astype(BF16)
    eye = (ri == ci).astype(F32)
    SW = RWKV_SUM_W
    bd = (_iota((SW, SW), 0) // N == _iota((SW, SW), 1) // N).astype(BF16)
    row0_w = _iota((C, W), 0) == 0
    row0_l = _iota((C, LW), 0) == 0

    def head_sum(x):
        xb = x.astype(BF16)
        return jnp.concatenate([_dot(xb[:, g:g + SW], bd) for g in range(0, W, SW)], axis=1)

    w0 = vec_ref[0:1, :]
    a0 = vec_ref[1:2, :]
    k_k = vec_ref[2:3, :]
    k_a = vec_ref[3:4, :]
    r_k = vec_ref[4:5, :]
    gn_g = vec_ref[5:6, :]
    gn_b = vec_ref[6:7, :]
    mu_r = vec_ref[8:9, :]
    mu_k = vec_ref[9:10, :]
    mu_v = vec_ref[10:11, :]
    mu_l = mul_ref[0:1, :]

    def body(c, carry):
        r0 = pl.multiple_of(c * C, C)
        p0 = pl.multiple_of(jnp.maximum(r0 - BF16_ROWS, 0), BF16_ROWS)
        notfirst = jnp.where(c > 0, 1.0, 0.0).astype(F32)
        lo = _shift_lerp(z_ref, 0, LW, r0, p0, notfirst, mu_l, row0_l)
        r = _shift_lerp(z_ref, LW, W, r0, p0, notfirst, mu_r, row0_w)
        k = _shift_lerp(z_ref, LW + W, W, r0, p0, notfirst, mu_k, row0_w)
        v = _shift_lerp(z_ref, LW + 2 * W, W, r0, p0, notfirst, mu_v, row0_w)

        wd = lo[:, 0:128]
        ad = lo[:, 128:256]
        gd = lo[:, 256:512]
        w_log = -_softplus(-(w0 + _dot(jnp.tanh(wd).astype(BF16), w2_ref[...]))) - 0.5
        lw = -jnp.exp(w_log)
        a = _sigmoid(a0 + _dot(ad.astype(BF16), a2_ref[...]))
        g = _dot(_sigmoid(gd).astype(BF16), g2_ref[...])

        kk = k * k_k
        kk = kk / jnp.maximum(jnp.sqrt(head_sum(kk * kk)), 1e-12)
        kmod = k * (1.0 + (a - 1.0) * k_a)
        b = kk * a

        cum = sum(_dot(tri, part) for part in _split3(lw))
        cum_c = cum[C - 1:C, :]
        e_inv = jnp.exp(-cum)
        e_fin = jnp.exp(cum_c - cum)
        kkt = kk * jnp.exp(cum - lw)
        rt = r * jnp.exp(cum)
        bt = b * e_inv
        kt = kmod * e_inv
        nbh = -b * e_fin
        kh = kmod * e_fin
        pc = jnp.exp(cum_c)

        hs = range(H)
        sl = [slice(h * N, (h + 1) * N) for h in hs]
        cat = lambda top, bot, h: jnp.concatenate([top[:, sl[h]], bot[:, sl[h]]], axis=0).astype(BF16)
        lhs = [cat(kkt, rt, h) for h in hs]
        rhs = [cat(bt, kt, h) for h in hs]
        vb = [v[:, sl[h]].astype(BF16) for h in hs]
        s_old = [s_ref[h] for h in hs]
        mall = [_dot_nt(lhs[h], rhs[h]) for h in hs]
        gm = [_dot_nt(lhs[h], s_old[h].astype(BF16)) for h in hs]
        mb = [jnp.where(incl, mall[h][C:, :C], 0.0).astype(BF16) for h in hs]
        akmk = [jnp.concatenate([jnp.where(strict, mall[h][:C, C:], 0.0),
                                 jnp.where(incl, mall[h][C:, C:], 0.0)], axis=0).astype(BF16)
                for h in hs]
        av = [_dot(akmk[h], vb[h]) for h in hs]
        square = lambda xs: [_dot(xs[h], xs[h]).astype(BF16) for h in hs]
        x0 = [jnp.where(strict, -mall[h][:C, :C], 0.0) for h in hs]
        t = [eye + x0[h] for h in hs]
        xp = square([x0[h].astype(BF16) for h in hs])
        for level in range(5):
            xn = square(xp) if level < 4 else None
            t = [t[h] + _dot(t[h].astype(BF16), xp[h]) for h in hs]
            xp = xn
        u = [_dot(t[h].astype(BF16), (gm[h][:C] + av[h][:C]).astype(BF16)) for h in hs]
        o = [gm[h][C:] + av[h][C:] - _dot(mb[h], u[h].astype(BF16)) for h in hs]
        upd = [_dot_tn(jnp.concatenate([u[h].astype(BF16), vb[h]], axis=0), cat(nbh, kh, h))
               for h in hs]
        for h in hs:
            s_ref[h] = s_old[h] * pc[:, sl[h]] + upd[h]

        o = jnp.concatenate(o, axis=1)
        d = o - head_sum(o) * (1.0 / N)
        var = head_sum(d * d) * (1.0 / N)
        on = d * lax.rsqrt(var + RWKV_GN_EPS) * gn_g + gn_b
        bonus = head_sum(r * kmod * r_k) * v
        o_ref[0, pl.ds(r0, C), :] = ((on + bonus) * g).astype(o_ref.dtype)
        return carry

    lax.fori_loop(0, nchunk, body, 0)


def _rwkv(z, vecs, mul, w2, a2, g2, B, LP):
    W = vecs.shape[1]
    ZW = z.shape[2]
    whole = lambda shape: pl.BlockSpec(shape, lambda b: (0,) * len(shape))
    return pl.pallas_call(
        _rwkv_kernel,
        grid=(B,),
        in_specs=[pl.BlockSpec((1, LP, ZW), lambda b: (b, 0, 0)),
                  whole((16, W)), whole((8, RWKV_LORA_W)), whole((LANES, W)), whole((LANES, W)),
                  whole((RWKV_GATE_RANK, W))],
        out_specs=pl.BlockSpec((1, LP, W), lambda b: (b, 0, 0)),
        out_shape=jax.ShapeDtypeStruct((B, LP, W), BF16),
        scratch_shapes=[pltpu.VMEM((W // RWKV_HEAD_DIM, RWKV_HEAD_DIM, RWKV_HEAD_DIM), F32)],
        compiler_params=_cparams(("parallel",)),
        name="rwkv7",
    )(z, vecs, mul, w2, a2, g2)


def _gate_prep_kernel(zs_ref, bias_ref, ct_ref, gr_ref, gc_ref):
    LP = zs_ref.shape[1]
    nkb = ct_ref.shape[2]
    TB, C = ATT_BLOCK, CHUNK
    ct_ref[...] = jnp.zeros_like(ct_ref)
    gr_ref[...] = jnp.zeros_like(gr_ref)
    sel = (_iota((16, LANES), 0) == _iota((16, LANES), 1)).astype(F32)
    sel_m = (_iota((SUBLANES, LANES), 1)
             == _iota((SUBLANES, LANES), 0) + MLSTM_GATE_LANE_I).astype(F32)
    carry = jnp.zeros((16, 1), F32)
    for kb in range(nkb):
        n = min(TB, LP - kb * TB)
        gz = zs_ref[0, kb * TB:kb * TB + n, :] + bias_ref[...]
        lf = _log_sigmoid(gz)
        ri = _iota((n, n), 0)
        ci = _iota((n, n), 1)
        cs = _dot(_dot_nt(sel, lf, HI), (ri <= ci).astype(F32), HI) + carry
        carry = cs[:, n - 1:n]
        for hp in range(ct_ref.shape[1]):
            ct_ref[0, hp, kb, 0:2, 0:n] = cs[2 * hp:2 * hp + 2, :]
        same = ri // C == ci // C
        x = jnp.where(_iota((n, LANES), 1) < MLSTM_GATE_LANE_F, gz, lf)
        xt = _dot_nt(sel_m, x, HI)
        bt = _dot(xt, ((ri <= ci) & same).astype(F32), HI)
        rows = jnp.where(_iota((SUBLANES, n), 0) < MLSTM_HEADS, xt, bt)
        for q in range(n // C):
            gr_ref[0, kb * (TB // C) + q, :, 0:C] = rows[:, q * C:(q + 1) * C]
        bc = _dot(((ri >= ci) & same).astype(F32), x, HI)
        gc_ref[0, kb * TB:kb * TB + n, :] = jnp.where(
            _iota((n, LANES), 1) < MLSTM_GATE_LANE_F, x, bc)


def _gate_prep(zs, bias, B, LP, n_pairs):
    nkb = pl.cdiv(LP, ATT_BLOCK)
    nchunk = LP // CHUNK
    return pl.pallas_call(
        _gate_prep_kernel,
        grid=(B,),
        in_specs=[pl.BlockSpec((1, LP, LANES), lambda b: (b, 0, 0)),
                  pl.BlockSpec((1, LANES), lambda b: (0, 0))],
        out_specs=[pl.BlockSpec((1, n_pairs, nkb, SUBLANES, ATT_BLOCK), lambda b: (b, 0, 0, 0, 0)),
                   pl.BlockSpec((1, nchunk, SUBLANES, LANES), lambda b: (b, 0, 0, 0)),
                   pl.BlockSpec((1, LP, LANES), lambda b: (b, 0, 0))],
        out_shape=[jax.ShapeDtypeStruct((B, n_pairs, nkb, SUBLANES, ATT_BLOCK), F32),
                   jax.ShapeDtypeStruct((B, nchunk, SUBLANES, LANES), F32),
                   jax.ShapeDtypeStruct((B, LP, LANES), F32)],
        compiler_params=_cparams(("parallel",)),
        name="gate_prep",
    )(zs, bias)


def _fox_kernel(q_ref, k_ref, v_ref, ct_ref, o_ref):
    TB, N = ATT_BLOCK, FOX_HEAD_DIM
    LP = k_ref.shape[1]
    n_full = LP // TB
    tail = LP - n_full * TB
    qi = pl.program_id(2)

    def attend(rows):
        halves = 2 if rows == TB else 1
        HQ = rows // halves
        units = [(h, s) for h in range(2) for s in range(halves)]
        nu = len(units)
        lane_q = _iota((HQ, LANES), 1)
        scale = N ** -0.5
        qu = [jnp.where((lane_q < N) == (h == 0), q_ref[0, s * HQ:(s + 1) * HQ, :] * scale, 0.0)
              for h, s in units]

        def block(k0, n, cks, masks, state):
            kblk = k_ref[0, pl.ds(k0, n), :]
            vblk = v_ref[0, pl.ds(k0, n), :]
            lane_v = _iota((n, LANES), 1)
            vaug = [jnp.where((lane_v < N) == (h == 0), vblk, 1.0) for h in range(2)]
            sc = [_dot_nt(qu[i], kblk) - cks[h] for i, (h, s) in enumerate(units)]
            if masks is not None:
                sc = [jnp.where(masks[s], sc[i], -jnp.inf) for i, (h, s) in enumerate(units)]
            m_new = [jnp.maximum(state[i][0], jnp.max(sc[i], axis=-1, keepdims=True))
                     for i in range(nu)]
            p = [jnp.exp(sc[i] - m_new[i]).astype(BF16) for i in range(nu)]
            pv = [_dot(p[i], vaug[h]) for i, (h, s) in enumerate(units)]
            return tuple((m_new[i], jnp.exp(state[i][0] - m_new[i]) * state[i][1] + pv[i])
                         for i in range(nu))

        ck = lambda kb, h: ct_ref[0, 0, kb, h:h + 1, :]

        def pair_body(kp, state):
            cks = [jnp.concatenate([ck(2 * kp, h), ck(2 * kp + 1, h)], axis=1) for h in range(2)]
            return block(pl.multiple_of(kp * 2 * TB, 2 * TB), 2 * TB, cks, None, state)

        def single_body(_, state):
            kb = qi - 1
            return block(pl.multiple_of(kb * TB, TB), TB, [ck(kb, h) for h in range(2)], None,
                         state)

        state = tuple((jnp.full((HQ, 1), NEG_BIG, F32), jnp.zeros((HQ, LANES), F32))
                      for _ in units)
        state = lax.fori_loop(0, lax.shift_right_logical(qi, 1), pair_body, state)
        state = lax.fori_loop(0, qi & 1, single_body, state)
        cks = [ck(qi, h)[:, 0:rows] for h in range(2)]
        masks = [_iota((HQ, rows), 0) + s * HQ >= _iota((HQ, rows), 1) for s in range(halves)]
        final = block(pl.multiple_of(qi * TB, TB), rows, cks, masks, state)
        for s in range(halves):
            acc0, acc1 = final[s][1], final[halves + s][1]
            out = jnp.where(lane_q < N, acc0 / acc0[:, N:N + 1], acc1 / acc1[:, 0:1])
            o_ref[0, s * HQ:(s + 1) * HQ, :] = out.astype(o_ref.dtype)

    if tail == 0:
        attend(TB)
    else:
        pl.when(qi < n_full)(lambda: attend(TB))
        pl.when(qi == n_full)(lambda: attend(tail))


def _fox(zq, ct, B, LP, width):
    n_pairs = width // LANES
    nq = pl.cdiv(LP, ATT_BLOCK)
    nkb = ct.shape[2]
    kv = lambda off: pl.BlockSpec((1, LP, LANES), lambda b, p, i: (b, 0, off + p))
    return pl.pallas_call(
        _fox_kernel,
        grid=(B, n_pairs, nq),
        in_specs=[pl.BlockSpec((1, ATT_BLOCK, LANES), lambda b, p, i: (b, i, p)),
                  kv(n_pairs), kv(2 * n_pairs),
                  pl.BlockSpec((1, 1, nkb, SUBLANES, ATT_BLOCK), lambda b, p, i: (b, p, 0, 0, 0))],
        out_specs=pl.BlockSpec((1, ATT_BLOCK, LANES), lambda b, p, i: (b, i, p)),
        out_shape=jax.ShapeDtypeStruct((B, LP, width), BF16),
        compiler_params=_cparams(("parallel", "parallel", "arbitrary")),
        name="fox_attention",
    )(zq, zq, zq, ct)


MLSTM_GATE_LANE_I = 12
MLSTM_GATE_LANE_F = 16


def _conv_silu(ref, r0, p0, notfirst, w_ref):
    cur = ref[0, pl.ds(r0, CHUNK), :].astype(F32)
    prev = ref[0, pl.ds(p0, BF16_ROWS), :][BF16_ROWS - SUBLANES:, :].astype(F32) * notfirst
    row8 = _iota(prev.shape, 0)
    y = cur * w_ref[MLSTM_CONV - 1:MLSTM_CONV, :]
    for s in range(1, MLSTM_CONV):
        rolled = pltpu.roll(cur, s, 0)
        top = jnp.where(row8 < s, pltpu.roll(prev, s, 0), rolled[0:SUBLANES])
        shifted = jnp.concatenate([top, rolled[SUBLANES:]], axis=0)
        y = y + shifted * w_ref[MLSTM_CONV - 1 - s:MLSTM_CONV - s, :]
    return y * _sigmoid(y)


def _mlstm_kernel(q_ref, k_ref, v_ref, og_ref, gr_ref, gc_ref, cwq_ref, cwk_ref,
                  o_ref, c_ref, m_ref):
    C, H = CHUNK, MLSTM_HEADS
    Dh = q_ref.shape[2] // H
    nchunk = q_ref.shape[1] // C
    c_ref[...] = jnp.zeros_like(c_ref)
    m_ref[...] = jnp.zeros_like(m_ref)

    incl = _iota((C, C), 0) >= _iota((C, C), 1)
    ones_col = (_iota((C, LANES), 1) == 0).astype(BF16)

    def body(c, carry):
        r0 = pl.multiple_of(c * C, C)
        p0 = pl.multiple_of(jnp.maximum(r0 - BF16_ROWS, 0), BF16_ROWS)
        notfirst = jnp.where(c > 0, 1.0, 0.0).astype(F32)
        q = _conv_silu(q_ref, r0, p0, notfirst, cwq_ref)
        k = _conv_silu(k_ref, r0, p0, notfirst, cwk_ref) * (Dh ** -0.5)
        v = v_ref[0, pl.ds(r0, C), :]
        og = og_ref[0, pl.ds(r0, C), :].astype(F32)
        g_rows = gr_ref[0, c]
        g_cols = gc_ref[0, pl.ds(r0, C), :]

        hs = range(H)
        sl = [slice(h * Dh, (h + 1) * Dh) for h in hs]
        m_all = m_ref[...]
        m_old = [m_all[h:h + 1, 0:1] for h in hs]
        c_old = [c_ref[h] for h in hs]
        li_r = [g_rows[h:h + 1, 0:C] for h in hs]
        b_r = [g_rows[H + h:H + h + 1, 0:C] for h in hs]
        li_c = [g_cols[:, MLSTM_GATE_LANE_I + h:MLSTM_GATE_LANE_I + h + 1] for h in hs]
        b_c = [g_cols[:, MLSTM_GATE_LANE_F + h:MLSTM_GATE_LANE_F + h + 1] for h in hs]
        qh = [q[:, sl[h]].astype(BF16) for h in hs]
        kh = [k[:, sl[h]] for h in hs]
        vaug = [jnp.concatenate([v[:, sl[h]], ones_col], axis=1) for h in hs]
        qk = [_dot_nt(qh[h], kh[h].astype(BF16)) for h in hs]
        qc = [_dot(qh[h], c_old[h].astype(BF16)) for h in hs]
        log_d = [jnp.where(incl, b_c[h] - b_r[h] + li_r[h], -jnp.inf) for h in hs]
        log_inter = [b_c[h] + m_old[h] for h in hs]
        m_q = [jnp.maximum(log_inter[h], jnp.max(log_d[h], axis=-1, keepdims=True)) for h in hs]
        s = [(qk[h] * jnp.exp(log_d[h] - m_q[h])).astype(BF16) for h in hs]
        b_end = [b_c[h][C - 1:C, :] for h in hs]
        m_new = [jnp.maximum(b_end[h] + m_old[h],
                             jnp.max(b_end[h] - b_r[h] + li_r[h], axis=-1, keepdims=True))
                 for h in hs]
        gk = [jnp.exp(b_end[h] - b_c[h] + li_c[h] - m_new[h]) for h in hs]
        num = [jnp.exp(log_inter[h] - m_q[h]) * qc[h] + _dot(s[h], vaug[h]) for h in hs]
        kv = [_dot_tn((gk[h] * kh[h]).astype(BF16), vaug[h]) for h in hs]
        outs = [_sigmoid(og[:, sl[h]]) * num[h][:, :Dh]
                / jnp.maximum(jnp.abs(num[h][:, Dh:Dh + 1]), jnp.exp(-m_q[h])) for h in hs]
        o_ref[0, pl.ds(r0, C), :] = jnp.concatenate(outs, axis=1).astype(o_ref.dtype)
        for h in hs:
            c_ref[h] = jnp.exp(b_end[h] + m_old[h] - m_new[h]) * c_old[h] + kv[h]
        m_ref[0:H, :] = jnp.concatenate([jnp.broadcast_to(m_new[h], (1, LANES)) for h in hs],
                                        axis=0)
        return carry

    lax.fori_loop(0, nchunk, body, 0)


def _mlstm(zm, g_rows, g_cols, cwq, cwk, B, LP):
    W = zm.shape[2] // 4
    Dh = W // MLSTM_HEADS
    seq = lambda j: pl.BlockSpec((1, LP, W), lambda b: (b, 0, j))
    cw = pl.BlockSpec((SUBLANES, W), lambda b: (0, 0))
    return pl.pallas_call(
        _mlstm_kernel,
        grid=(B,),
        in_specs=[seq(0), seq(1), seq(2), seq(3),
                  pl.BlockSpec((1, LP // CHUNK, SUBLANES, LANES), lambda b: (b, 0, 0, 0)),
                  pl.BlockSpec((1, LP, LANES), lambda b: (b, 0, 0)), cw, cw],
        out_specs=pl.BlockSpec((1, LP, W), lambda b: (b, 0, 0)),
        out_shape=jax.ShapeDtypeStruct((B, LP, W), BF16),
        scratch_shapes=[pltpu.VMEM((MLSTM_HEADS, Dh, 2 * Dh), F32),
                        pltpu.VMEM((SUBLANES, LANES), F32)],
        compiler_params=_cparams(("parallel",)),
        name="mlstm",
    )(zm, zm, zm, zm, g_rows, g_cols, cwq, cwk)


def _pad_rows(w, n):
    return jnp.pad(w, ((0, n - w.shape[0]), (0, 0)))


def kernel(x, meta_tokens, ln_emb_g, ln_emb_b, w_in, rwkv_mu, rwkv_w0, rwkv_w2, rwkv_a0, rwkv_a2, rwkv_g2, rwkv_k_k, rwkv_k_a, rwkv_r_k, rwkv_gn_g, rwkv_gn_b, fox_b_f, mlstm_conv_w, mlstm_b_i, mlstm_b_f, proj_rwkv, proj_fox, proj_mlstm, w_out, ln1_g, ln1_b, ffn_w1, ffn_w3, ffn_w2, router_w, moe_w1, moe_w3, moe_w2, ln2_g, ln2_b):
    B, S, D = x.shape
    L = N_META + S
    LP = -(-L // CHUNK) * CHUNK
    T = B * LP
    RW = proj_rwkv.shape[1]
    FW = proj_fox.shape[1]
    MW = proj_mlstm.shape[1]
    n_fox_heads = FW // FOX_HEAD_DIM
    dr, ar, gr = RWKV_DECAY_RANK, RWKV_AAA_RANK, RWKV_GATE_RANK
    c1 = 3 * RW + dr + ar + gr

    meta = jnp.broadcast_to(meta_tokens[None].astype(x.dtype), (B, N_META, D))
    hin = jnp.concatenate([meta, x, jnp.zeros((B, LP - L, D), x.dtype)], axis=1).reshape(T, D)
    h, hb = _ln(hin, ln_emb_g, ln_emb_b)

    ffn_b = tuple(w.astype(BF16) for w in (ffn_w1, ffn_w3, ffn_w2))
    moe_b = tuple(w.astype(BF16) for w in (moe_w1, moe_w3, moe_w2))

    for l in range(DEPTH):
        w_r, w_f, w_m, w_s, w_g = _prep_w_in(w_in, l, RW, FW, MW, dr, ar, gr)

        z_r = _mm(hb, w_r, BF16, w_r.shape[1], "in_rwkv").reshape(B, LP, -1)
        z_f = _mm(hb, w_f, BF16, w_f.shape[1], "in_fox").reshape(B, LP, -1)
        z_m = _mm(hb, w_m, BF16, w_m.shape[1], "in_mlstm").reshape(B, LP, -1)
        z_s = _mm(hb, w_s, F32, LANES, "in_gates").reshape(B, LP, LANES)

        mu = rwkv_mu[l]
        vecs = jnp.zeros((16, RW), F32)
        for i, p in enumerate([rwkv_w0[l], rwkv_a0[l], rwkv_k_k[l], rwkv_k_a[l],
                               rwkv_r_k[l].reshape(RW), rwkv_gn_g[l], rwkv_gn_b[l]]):
            vecs = vecs.at[i].set(p)
        vecs = vecs.at[8].set(mu[:RW]).at[9].set(mu[RW:2 * RW]).at[10].set(mu[2 * RW:3 * RW])
        mul = jnp.zeros((8, RWKV_LORA_W), F32)
        mul = mul.at[0, 0:dr].set(mu[3 * RW:3 * RW + dr])
        mul = mul.at[0, LANES:LANES + ar].set(mu[3 * RW + dr:3 * RW + dr + ar])
        mul = mul.at[0, 2 * LANES:2 * LANES + gr].set(mu[3 * RW + dr + ar:c1])
        y_r = _rwkv(z_r, vecs, mul, _pad_rows(rwkv_w2[l], LANES).astype(BF16),
                    _pad_rows(rwkv_a2[l], LANES).astype(BF16), rwkv_g2[l].astype(BF16), B, LP)

        bias = jnp.zeros((1, LANES), F32)
        bias = bias.at[0, 0:n_fox_heads].set(fox_b_f[l])
        bias = bias.at[0, MLSTM_GATE_LANE_I:MLSTM_GATE_LANE_I + MLSTM_HEADS].set(mlstm_b_i[l])
        bias = bias.at[0, MLSTM_GATE_LANE_F:MLSTM_GATE_LANE_F + MLSTM_HEADS].set(mlstm_b_f[l])

        ct, g_rows, g_cols = _gate_prep(z_s, bias, B, LP, FW // LANES)
        y_f = _fox(z_f, ct, B, LP, FW)

        cw = _pad_rows(mlstm_conv_w[l], SUBLANES)
        y_m = _mlstm(z_m, g_rows, g_cols, cw[:, :MW], cw[:, MW:], B, LP)

        pre = _merge(hb, w_g, y_r.reshape(T, RW), y_f.reshape(T, FW), y_m.reshape(T, MW),
                     proj_rwkv[l].astype(BF16), proj_fox[l].astype(BF16),
                     proj_mlstm[l].astype(BF16))
        h, hb = _mm_res_ln(pre, w_out[l].astype(BF16), h, ln1_g[l], ln1_b[l])

        i = l // 2
        if l % 2 == 0:
            h, hb = _ffn(hb, ffn_b, i, h, ln2_g[l], ln2_b[l])
        else:
            tm = _tile(T, MOE_TILE)
            n_tiles = 2 * T // tm + N_EXPERTS
            gw, ids = _router(h, router_w[i])
            row_token, dest, tile_expert, n_used = _route(ids, tm, n_tiles)
            xs = _gather_rows(h, row_token)
            ys = _moe_ffn(xs, moe_b, i, tile_expert, n_used, tm)
            h, hb = _combine(ys, dest, gw, h, ln2_g[l], ln2_b[l])

    return h.reshape(B, LP, D)[:, N_META:L]
```

```python
import functools
import math

import jax
import jax.numpy as jnp
from jax import lax
from jax.experimental import pallas as pl
from jax.experimental.pallas import tpu as pltpu

F32 = jnp.float32
BF16 = jnp.bfloat16
HI = lax.Precision.HIGHEST

N_META = 16
DEPTH = 4
RWKV_HEAD_DIM = 64
RWKV_DECAY_RANK = 96
RWKV_AAA_RANK = 96
RWKV_GATE_RANK = 256
RWKV_GN_EPS = 64e-5
FOX_HEAD_DIM = 64
MLSTM_HEADS = 4
MLSTM_CONV = 4
N_EXPERTS = 8
LN_EPS = 1e-5
DEEPNORM_ALPHA = (2 * DEPTH) ** 0.25

CHUNK = 64
LANES = 128
SUBLANES = 8
BF16_ROWS = 16
ATT_BLOCK = 512
GATE_PREP_BLOCK = 256
FFN_ROWS = 512
MOE_TILE = 1024
GATHER_TILE = 512
DMA_UNROLL = 8
VMEM_LIMIT = 56 * 1024 * 1024
NEG_BIG = -1e30


def _cparams(sem):
    return pltpu.CompilerParams(dimension_semantics=sem, vmem_limit_bytes=VMEM_LIMIT)


def _tile(n, pref, mult=SUBLANES):
    best = None
    for d in range(mult, min(n, pref) + 1, mult):
        if n % d == 0:
            best = d
    return best if best is not None else n


def _dot(a, b, precision=None):
    return jnp.dot(a, b, preferred_element_type=F32, precision=precision)


def _dot_nt(a, b, precision=None):
    return lax.dot_general(a, b, (((1,), (1,)), ((), ())), preferred_element_type=F32,
                           precision=precision)


def _dot_tn(a, b, precision=None):
    return lax.dot_general(a, b, (((0,), (0,)), ((), ())), preferred_element_type=F32,
                           precision=precision)


def _softplus(x):
    return jnp.maximum(x, 0.0) + jnp.log(1.0 + jnp.exp(-jnp.abs(x)))


def _log_sigmoid(x):
    return jnp.minimum(x, 0.0) - jnp.log(1.0 + jnp.exp(-jnp.abs(x)))


def _sigmoid(x):
    return 1.0 / (1.0 + jnp.exp(-x))


def _iota(shape, dim):
    return lax.broadcasted_iota(jnp.int32, shape, dim)


def _layer_norm(t, g, b):
    mu = jnp.mean(t, axis=-1, keepdims=True)
    d = t - mu
    var = jnp.mean(d * d, axis=-1, keepdims=True)
    return d * lax.rsqrt(var + LN_EPS) * g + b


def _ln_kernel(x_ref, g_ref, b_ref, o_ref, ob_ref):
    y = _layer_norm(x_ref[...], g_ref[...], b_ref[...])
    o_ref[...] = y
    ob_ref[...] = y.astype(BF16)


def _ln(x, g, b):
    T, D = x.shape
    bm = _tile(T, 512)
    return pl.pallas_call(
        _ln_kernel,
        grid=(T // bm,),
        in_specs=[pl.BlockSpec((bm, D), lambda i: (i, 0)),
                  pl.BlockSpec((1, D), lambda i: (0, 0)),
                  pl.BlockSpec((1, D), lambda i: (0, 0))],
        out_specs=[pl.BlockSpec((bm, D), lambda i: (i, 0)),
                   pl.BlockSpec((bm, D), lambda i: (i, 0))],
        out_shape=[jax.ShapeDtypeStruct((T, D), F32), jax.ShapeDtypeStruct((T, D), BF16)],
        compiler_params=_cparams(("parallel",)),
        name="ln_embed",
    )(x, g.reshape(1, D), b.reshape(1, D))


def _prep_w_in_kernel(w_ref, or_ref, of_ref, om_ref, os_ref, og_ref, *, rw, fw, mw, dr, ar):
    x = w_ref[0]
    c1 = or_ref.shape[1] - 2 * LANES + dr + ar
    c2 = c1 + 3 * fw + fw // FOX_HEAD_DIM
    c3 = c2 + 4 * mw + 2 * MLSTM_HEADS
    zeros = lambda n: jnp.zeros((x.shape[0], n), F32)
    cast = lambda parts: jnp.concatenate(parts, axis=1).astype(BF16)
    or_ref[...] = cast([x[:, 3 * rw:3 * rw + dr], zeros(LANES - dr),
                        x[:, 3 * rw + dr:3 * rw + dr + ar], zeros(LANES - ar),
                        x[:, 3 * rw + dr + ar:c1], x[:, 0:3 * rw]])
    of_ref[...] = x[:, c1:c1 + 3 * fw].astype(BF16)
    om_ref[...] = x[:, c2:c2 + 4 * mw].astype(BF16)
    n_small = c2 - (c1 + 3 * fw) + c3 - (c2 + 4 * mw)
    os_ref[...] = cast([x[:, c1 + 3 * fw:c2], x[:, c2 + 4 * mw:c3], zeros(LANES - n_small)])
    og_ref[...] = x[:, c3:].astype(BF16)


def _prep_w_in(w_in, l, rw, fw, mw, dr, ar, gr):
    _, D, n_in = w_in.shape
    rows = _tile(D, 128, BF16_ROWS)
    widths = [3 * rw + 2 * LANES + gr, 3 * fw, 4 * mw, LANES, 3 * D]
    return pl.pallas_call(
        functools.partial(_prep_w_in_kernel, rw=rw, fw=fw, mw=mw, dr=dr, ar=ar),
        grid=(D // rows,),
        in_specs=[pl.BlockSpec((1, rows, n_in), lambda i: (l, i, 0))],
        out_specs=[pl.BlockSpec((rows, w), lambda i: (i, 0)) for w in widths],
        out_shape=[jax.ShapeDtypeStruct((D, w), BF16) for w in widths],
        compiler_params=_cparams(("parallel",)),
        name="prep_w_in",
    )(w_in)


def _mm_kernel(a_ref, b_ref, o_ref):
    o_ref[...] = _dot(a_ref[...], b_ref[...]).astype(o_ref.dtype)


def _mm(a, b, out_dtype, bn_pref, name):
    M, K = a.shape
    N = b.shape[1]
    bm = _tile(M, 1024)
    bn = _tile(N, bn_pref, LANES)
    return pl.pallas_call(
        _mm_kernel,
        grid=(M // bm, N // bn),
        in_specs=[pl.BlockSpec((bm, K), lambda i, j: (i, 0)),
                  pl.BlockSpec((K, bn), lambda i, j: (0, j))],
        out_specs=pl.BlockSpec((bm, bn), lambda i, j: (i, j)),
        out_shape=jax.ShapeDtypeStruct((M, N), out_dtype),
        compiler_params=_cparams(("parallel", "arbitrary")),
        name=name,
    )(a, b)


def _mm_res_ln_kernel(a_ref, w_ref, h_ref, g_ref, b_ref, o_ref, ob_ref):
    t = DEEPNORM_ALPHA * h_ref[...] + _dot(a_ref[...], w_ref[...])
    y = _layer_norm(t, g_ref[...], b_ref[...])
    o_ref[...] = y
    ob_ref[...] = y.astype(BF16)


def _mm_res_ln(a, w, h, g, b):
    M, K = a.shape
    D = w.shape[1]
    bm = _tile(M, 512)
    return pl.pallas_call(
        _mm_res_ln_kernel,
        grid=(M // bm,),
        in_specs=[pl.BlockSpec((bm, K), lambda i: (i, 0)),
                  pl.BlockSpec((K, D), lambda i: (0, 0)),
                  pl.BlockSpec((bm, D), lambda i: (i, 0)),
                  pl.BlockSpec((1, D), lambda i: (0, 0)),
                  pl.BlockSpec((1, D), lambda i: (0, 0))],
        out_specs=[pl.BlockSpec((bm, D), lambda i: (i, 0)),
                   pl.BlockSpec((bm, D), lambda i: (i, 0))],
        out_shape=[jax.ShapeDtypeStruct((M, D), F32), jax.ShapeDtypeStruct((M, D), BF16)],
        compiler_params=_cparams(("parallel",)),
        name="out_proj_ln",
    )(a, w, h, g.reshape(1, D), b.reshape(1, D))


def _merge_kernel(hb_ref, wgr_ref, wgf_ref, wgm_ref, yr_ref, yf_ref, ym_ref, pr_ref, pf_ref,
                  pm_ref, o_ref):
    hb = hb_ref[...]
    acc = _sigmoid(_dot(hb, wgr_ref[...])) * _dot(yr_ref[...], pr_ref[...])
    acc += _sigmoid(_dot(hb, wgf_ref[...])) * _dot(yf_ref[...], pf_ref[...])
    acc += _sigmoid(_dot(hb, wgm_ref[...])) * _dot(ym_ref[...], pm_ref[...])
    o_ref[...] = acc.astype(o_ref.dtype)


def _merge(hb, wg, yr, yf, ym, pr, pf, pm):
    M, D = hb.shape
    bm = _tile(M, 1024)
    bn = _tile(D, 512, LANES)
    nb = D // bn
    row = lambda w: pl.BlockSpec((bm, w), lambda i, j: (i, 0))
    col = lambda k: pl.BlockSpec((k, bn), lambda i, j: (0, j))
    gate = lambda br: pl.BlockSpec((D, bn), lambda i, j: (0, br * nb + j))
    return pl.pallas_call(
        _merge_kernel,
        grid=(M // bm, nb),
        in_specs=[row(D), gate(0), gate(1), gate(2),
                  row(yr.shape[1]), row(yf.shape[1]), row(ym.shape[1]),
                  col(pr.shape[0]), col(pf.shape[0]), col(pm.shape[0])],
        out_specs=pl.BlockSpec((bm, bn), lambda i, j: (i, j)),
        out_shape=jax.ShapeDtypeStruct((M, D), BF16),
        compiler_params=_cparams(("parallel", "arbitrary")),
        name="merge",
    )(hb, wg, wg, wg, yr, yf, ym, pr, pf, pm)


def _router_kernel(h_ref, w_ref, ow_ref, oi_ref):
    logits = _dot(h_ref[...], w_ref[...], HI)
    lane = _iota(logits.shape, 1)
    logits = jnp.where(lane < N_EXPERTS, logits, -jnp.inf)
    m1 = jnp.max(logits, axis=-1, keepdims=True)
    i1 = jnp.min(jnp.where(logits == m1, lane, LANES), axis=-1, keepdims=True)
    rest = jnp.where(lane == i1, -jnp.inf, logits)
    m2 = jnp.max(rest, axis=-1, keepdims=True)
    i2 = jnp.min(jnp.where(rest == m2, lane, LANES), axis=-1, keepdims=True)
    e2 = jnp.exp(m2 - m1)
    den = 1.0 + e2
    ow_ref[...] = jnp.where(lane == 0, 1.0 / den, jnp.where(lane == 1, e2 / den, 0.0))
    oi_ref[...] = jnp.where(lane == 0, i1, jnp.where(lane == 1, i2, 0))


def _router(h, w):
    T, D = h.shape
    bm = _tile(T, 512)
    wp = jnp.zeros((D, LANES), F32).at[:, :N_EXPERTS].set(w)
    out = pl.BlockSpec((bm, LANES), lambda i: (i, 0))
    return pl.pallas_call(
        _router_kernel,
        grid=(T // bm,),
        in_specs=[pl.BlockSpec((bm, D), lambda i: (i, 0)),
                  pl.BlockSpec((D, LANES), lambda i: (0, 0))],
        out_specs=[out, out],
        out_shape=[jax.ShapeDtypeStruct((T, LANES), F32),
                   jax.ShapeDtypeStruct((T, LANES), jnp.int32)],
        compiler_params=_cparams(("parallel",)),
        name="router",
    )(h, wp)


def _route(ids, tm, n_tiles):
    T = ids.shape[0]
    e = ids[:, :2].reshape(-1)
    onehot = (e[:, None] == jnp.arange(N_EXPERTS, dtype=jnp.int32)[None, :]).astype(jnp.int32)
    pos = jnp.take_along_axis(jnp.cumsum(onehot, axis=0), e[:, None], axis=1)[:, 0] - 1
    counts = jnp.sum(onehot, axis=0)
    padded = (counts + tm - 1) // tm * tm
    ends = jnp.cumsum(padded)
    dest = (ends - padded)[e] + pos
    row_token = jnp.zeros((n_tiles * tm,), jnp.int32).at[dest].set(
        jnp.arange(2 * T, dtype=jnp.int32) // 2)
    tile_start = jnp.arange(n_tiles, dtype=jnp.int32) * tm
    tile_expert = jnp.minimum(
        jnp.sum((ends[None, :] <= tile_start[:, None]).astype(jnp.int32), axis=1), N_EXPERTS - 1)
    n_used = (ends[-1] // tm).astype(jnp.int32).reshape(1)
    return row_token.reshape(n_tiles, 1, tm), dest, tile_expert, n_used


def _row_copy(src_ref, buf_ref, sem, src_row, dst_row):
    return pltpu.make_async_copy(src_ref.at[pl.ds(src_row, 1), :],
                                 buf_ref.at[pl.ds(dst_row, 1), :], sem.at[0])


def _gather_kernel(idx_ref, src_ref, o_ref, buf_ref, sem):
    tm = buf_ref.shape[0]

    def issue(r, c):
        _row_copy(src_ref, buf_ref, sem, idx_ref[0, 0, r], r).start()
        return c

    def drain(r, c):
        _row_copy(src_ref, buf_ref, sem, 0, r).wait()
        return c

    lax.fori_loop(0, tm, issue, 0, unroll=DMA_UNROLL)
    lax.fori_loop(0, tm, drain, 0, unroll=DMA_UNROLL)
    o_ref[...] = buf_ref[...].astype(o_ref.dtype)


def _gather_rows(src, row_token):
    tm = _tile(row_token.shape[2], GATHER_TILE)
    row_token = row_token.reshape(-1, 1, tm)
    n_tiles = row_token.shape[0]
    D = src.shape[1]
    return pl.pallas_call(
        _gather_kernel,
        grid=(n_tiles,),
        in_specs=[pl.BlockSpec((1, 1, tm), lambda i: (i, 0, 0), memory_space=pltpu.SMEM),
                  pl.BlockSpec(memory_space=pl.ANY)],
        out_specs=pl.BlockSpec((tm, D), lambda i: (i, 0)),
        out_shape=jax.ShapeDtypeStruct((n_tiles * tm, D), BF16),
        scratch_shapes=[pltpu.VMEM((tm, D), F32), pltpu.SemaphoreType.DMA((1,))],
        compiler_params=_cparams(("arbitrary",)),
        name="moe_gather",
    )(row_token, src)


def _moe_ffn_kernel(te_ref, nu_ref, x_ref, w1_ref, w3_ref, w2_ref, o_ref):
    j = pl.program_id(1)

    @pl.when(j == 0)
    def _():
        o_ref[...] = jnp.zeros_like(o_ref)

    @pl.when(pl.program_id(0) < nu_ref[0])
    def _():
        _swiglu_accumulate(x_ref, w1_ref[0, 0], w3_ref[0, 0], w2_ref[0, 0], o_ref)


def _moe_ffn(xs, weights, layer, tile_expert, n_used, tm):
    w1, w3, w2 = weights
    NP, D = xs.shape
    FF = w1.shape[3]
    bf = _tile(FF, 512, LANES)
    grid_spec = pltpu.PrefetchScalarGridSpec(
        num_scalar_prefetch=2,
        grid=(NP // tm, FF // bf),
        in_specs=[pl.BlockSpec((tm, D), lambda i, j, te, nu: (i, 0)),
                  pl.BlockSpec((1, 1, D, bf), lambda i, j, te, nu: (layer, te[i], 0, j)),
                  pl.BlockSpec((1, 1, D, bf), lambda i, j, te, nu: (layer, te[i], 0, j)),
                  pl.BlockSpec((1, 1, bf, D), lambda i, j, te, nu: (layer, te[i], j, 0))],
        out_specs=pl.BlockSpec((tm, D), lambda i, j, te, nu: (i, 0)),
    )
    return pl.pallas_call(
        _moe_ffn_kernel,
        grid_spec=grid_spec,
        out_shape=jax.ShapeDtypeStruct((NP, D), F32),
        compiler_params=_cparams(("parallel", "arbitrary")),
        name="moe_ffn",
    )(tile_expert, n_used, xs, w1, w3, w2)


def _combine_kernel(d_ref, y_ref, gw_ref, h_ref, g_ref, b_ref, o_ref, ob_ref, buf_ref, sem):
    bm = buf_ref.shape[1]

    def issue(g, c):
        for u in range(DMA_UNROLL // 2):
            t = g * (DMA_UNROLL // 2) + u
            for r in range(2):
                _row_copy(y_ref, buf_ref.at[r], sem, d_ref[0, 0, 2 * t + r], t).start(priority=r)
        return c

    def drain(t, c):
        for r in range(2):
            _row_copy(y_ref, buf_ref.at[r], sem, 0, t).wait()
        return c

    lax.fori_loop(0, bm // (DMA_UNROLL // 2), issue, 0)
    lax.fori_loop(0, bm, drain, 0, unroll=DMA_UNROLL // 2)
    gw = gw_ref[...]
    ff = gw[:, 0:1] * buf_ref[0] + gw[:, 1:2] * buf_ref[1]
    y = _layer_norm(DEEPNORM_ALPHA * h_ref[...] + ff, g_ref[...], b_ref[...])
    o_ref[...] = y
    ob_ref[...] = y.astype(BF16)


def _combine(ys, dest, gw, h, g, b):
    T, D = h.shape
    bm = _tile(T, 256)
    row = lambda w: pl.BlockSpec((bm, w), lambda i: (i, 0))
    vec = pl.BlockSpec((1, D), lambda i: (0, 0))
    return pl.pallas_call(
        _combine_kernel,
        grid=(T // bm,),
        in_specs=[pl.BlockSpec((1, 1, 2 * bm), lambda i: (i, 0, 0), memory_space=pltpu.SMEM),
                  pl.BlockSpec(memory_space=pl.ANY), row(LANES), row(D), vec, vec],
        out_specs=[row(D), row(D)],
        out_shape=[jax.ShapeDtypeStruct((T, D), F32), jax.ShapeDtypeStruct((T, D), BF16)],
        scratch_shapes=[pltpu.VMEM((2, bm, D), F32), pltpu.SemaphoreType.DMA((1,))],
        compiler_params=_cparams(("arbitrary",)),
        name="moe_combine",
    )(dest.reshape(T // bm, 1, 2 * bm), ys, gw, h, g.reshape(1, D), b.reshape(1, D))


def _swiglu_accumulate(x_ref, w1, w3, w2, acc_ref):
    hm = x_ref.shape[0] // 2
    rows = [slice(0, hm), slice(hm, 2 * hm)]
    ac = [(_dot(x_ref[r, :], w1), _dot(x_ref[r, :], w3)) for r in rows]
    hh = [(a * _sigmoid(a) * c).astype(BF16) for a, c in ac]
    for r, t in zip(rows, hh):
        acc_ref[r, :] += _dot(t, w2)


def _ffn_kernel(x_ref, w1_ref, w3_ref, w2_ref, h_ref, g_ref, b_ref, o_ref, ob_ref, acc_ref):
    j = pl.program_id(1)

    @pl.when(j == 0)
    def _():
        acc_ref[...] = jnp.zeros_like(acc_ref)

    _swiglu_accumulate(x_ref, w1_ref[0], w3_ref[0], w2_ref[0], acc_ref)

    @pl.when(j == pl.num_programs(1) - 1)
    def _():
        y = _layer_norm(DEEPNORM_ALPHA * h_ref[...] + acc_ref[...], g_ref[...], b_ref[...])
        o_ref[...] = y
        ob_ref[...] = y.astype(BF16)


def _ffn(xb, weights, layer, h, g, b):
    w1, w3, w2 = weights
    M, D = xb.shape
    FF = w1.shape[2]
    bm = _tile(M, FFN_ROWS, 2 * BF16_ROWS)
    bf = _tile(FF, 512, LANES)
    row = lambda w: pl.BlockSpec((bm, w), lambda i, j: (i, 0))
    vec = pl.BlockSpec((1, D), lambda i, j: (0, 0))
    return pl.pallas_call(
        _ffn_kernel,
        grid=(M // bm, FF // bf),
        in_specs=[row(D),
                  pl.BlockSpec((1, D, bf), lambda i, j: (layer, 0, j)),
                  pl.BlockSpec((1, D, bf), lambda i, j: (layer, 0, j)),
                  pl.BlockSpec((1, bf, D), lambda i, j: (layer, j, 0)),
                  row(D), vec, vec],
        out_specs=[row(D), row(D)],
        out_shape=[jax.ShapeDtypeStruct((M, D), F32), jax.ShapeDtypeStruct((M, D), BF16)],
        scratch_shapes=[pltpu.VMEM((bm, D), F32)],
        compiler_params=_cparams(("parallel", "arbitrary")),
        name="ffn",
    )(xb, w1, w3, w2, h, g.reshape(1, D), b.reshape(1, D))


RWKV_LORA_W = 512
RWKV_SUM_W = 256


def _shift_lerp(ref, c0, width, r0, p0, notfirst, mu, row0):
    cur = ref[0, pl.ds(r0, CHUNK), c0:c0 + width].astype(F32)
    prev = ref[0, pl.ds(p0, BF16_ROWS), c0:c0 + width][BF16_ROWS - 1:BF16_ROWS, :].astype(F32)
    zprev = jnp.where(row0, prev * notfirst, pltpu.roll(cur, 1, 0))
    return cur + mu * (zprev - cur)


def _split3(x):
    hi = x.astype(BF16)
    r1 = x - hi.astype(F32)
    mid = r1.astype(BF16)
    lo = (r1 - mid.astype(F32)).astype(BF16)
    return hi, mid, lo


def _rwkv_kernel(z_ref, vec_ref, mul_ref, w2_ref, a2_ref, g2_ref, o_ref, s_ref):
    C, N = CHUNK, RWKV_HEAD_DIM
    W = o_ref.shape[2]
    H = W // N
    LW = RWKV_LORA_W
    nchunk = z_ref.shape[1] // C
    s_ref[...] = jnp.zeros_like(s_ref)

    ri = _iota((C, C), 0)
    ci = _iota((C, C), 1)
    strict = ri > ci
    incl = ri >= ci
    tri = incl.astype(BF16)
    eye = (ri == ci).astype(F32)
    SW = RWKV_SUM_W
    bd = (_iota((SW, SW), 0) // N == _iota((SW, SW), 1) // N).astype(BF16)
    row0_w = _iota((C, W), 0) == 0
    row0_l = _iota((C, LW), 0) == 0

    def head_sum(x):
        xb = x.astype(BF16)
        return jnp.concatenate([_dot(xb[:, g:g + SW], bd) for g in range(0, W, SW)], axis=1)

    w0 = vec_ref[0:1, :]
    a0 = vec_ref[1:2, :]
    k_k = vec_ref[2:3, :]
    k_a = vec_ref[3:4, :]
    r_k = vec_ref[4:5, :]
    gn_g = vec_ref[5:6, :]
    gn_b = vec_ref[6:7, :]
    mu_r = vec_ref[8:9, :]
    mu_k = vec_ref[9:10, :]
    mu_v = vec_ref[10:11, :]
    mu_l = mul_ref[0:1, :]

    def body(c, carry):
        r0 = pl.multiple_of(c * C, C)
        p0 = pl.multiple_of(jnp.maximum(r0 - BF16_ROWS, 0), BF16_ROWS)
        notfirst = jnp.where(c > 0, 1.0, 0.0).astype(F32)
        lo = _shift_lerp(z_ref, 0, LW, r0, p0, notfirst, mu_l, row0_l)
        r = _shift_lerp(z_ref, LW, W, r0, p0, notfirst, mu_r, row0_w)
        k = _shift_lerp(z_ref, LW + W, W, r0, p0, notfirst, mu_k, row0_w)
        v = _shift_lerp(z_ref, LW + 2 * W, W, r0, p0, notfirst, mu_v, row0_w)

        wd = lo[:, 0:128]
        ad = lo[:, 128:256]
        gd = lo[:, 256:512]
        w_log = -_softplus(-(w0 + _dot(jnp.tanh(wd).astype(BF16), w2_ref[...]))) - 0.5
        lw = -jnp.exp(w_log)
        a = _sigmoid(a0 + _dot(ad.astype(BF16), a2_ref[...]))
        g = _dot(_sigmoid(gd).astype(BF16), g2_ref[...])

        kk = k * k_k
        kk = kk / jnp.maximum(jnp.sqrt(head_sum(kk * kk)), 1e-12)
        kmod = k * (1.0 + (a - 1.0) * k_a)
        b = kk * a

        cum = sum(_dot(tri, part) for part in _split3(lw))
        cum_c = cum[C - 1:C, :]
        e_inv = jnp.exp(-cum)
        e_fin = jnp.exp(cum_c - cum)
        kkt = kk * jnp.exp(cum - lw)
        rt = r * jnp.exp(cum)
        bt = b * e_inv
        kt = kmod * e_inv
        nbh = -b * e_fin
        kh = kmod * e_fin
        pc = jnp.exp(cum_c)

        hs = range(H)
        sl = [slice(h * N, (h + 1) * N) for h in hs]
        cat = lambda top, bot, h: jnp.concatenate([top[:, sl[h]], bot[:, sl[h]]], axis=0).astype(BF16)
        lhs = [cat(kkt, rt, h) for h in hs]
        rhs = [cat(bt, kt, h) for h in hs]
        vb = [v[:, sl[h]].astype(BF16) for h in hs]
        s_old = [s_ref[h] for h in hs]
        mall = [_dot_nt(lhs[h], rhs[h]) for h in hs]
        gm = [_dot_nt(lhs[h], s_old[h].astype(BF16)) for h in hs]
        mb = [jnp.where(incl, mall[h][C:, :C], 0.0).astype(BF16) for h in hs]
        akmk = [jnp.concatenate([jnp.where(strict, mall[h][:C, C:], 0.0),
                                 jnp.where(incl, mall[h][C:, C:], 0.0)], axis=0).astype(BF16)
                for h in hs]
        av = [_dot(akmk[h], vb[h]) for h in hs]
        square = lambda xs: [_dot(xs[h], xs[h]).astype(BF16) for h in hs]
        x0 = [jnp.where(strict, -mall[h][:C, :C], 0.0) for h in hs]
        t = [eye + x0[h] for h in hs]
        xp = square([x0[h].astype(BF16) for h in hs])
        for level in range(5):
            xn = square(xp) if level < 4 else None
            t = [t[h] + _dot(t[h].astype(BF16), xp[h]) for h in hs]
            xp = xn
        u = [_dot(t[h].astype(BF16), (gm[h][:C] + av[h][:C]).astype(BF16)) for h in hs]
        o = [gm[h][C:] + av[h][C:] - _dot(mb[h], u[h].astype(BF16)) for h in hs]
        upd = [_dot_tn(jnp.concatenate([u[h].astype(BF16), vb[h]], axis=0), cat(nbh, kh, h))
               for h in hs]
        for h in hs:
            s_ref[h] = s_old[h] * pc[:, sl[h]] + upd[h]

        o = jnp.concatenate(o, axis=1)
        d = o - head_sum(o) * (1.0 / N)
        var = head_sum(d * d) * (1.0 / N)
        on = d * lax.rsqrt(var + RWKV_GN_EPS) * gn_g + gn_b
        bonus = head_sum(r * kmod * r_k) * v
        o_ref[0, pl.ds(r0, C), :] = ((on + bonus) * g).astype(o_ref.dtype)
        return carry

    lax.fori_loop(0, nchunk, body, 0)


def _rwkv(z, vecs, mul, w2, a2, g2, B, LP):
    W = vecs.shape[1]
    ZW = z.shape[2]
    whole = lambda shape: pl.BlockSpec(shape, lambda b: (0,) * len(shape))
    return pl.pallas_call(
        _rwkv_kernel,
        grid=(B,),
        in_specs=[pl.BlockSpec((1, LP, ZW), lambda b: (b, 0, 0)),
                  whole((16, W)), whole((8, RWKV_LORA_W)), whole((LANES, W)), whole((LANES, W)),
                  whole((RWKV_GATE_RANK, W))],
        out_specs=pl.BlockSpec((1, LP, W), lambda b: (b, 0, 0)),
        out_shape=jax.ShapeDtypeStruct((B, LP, W), BF16),
        scratch_shapes=[pltpu.VMEM((W // RWKV_HEAD_DIM, RWKV_HEAD_DIM, RWKV_HEAD_DIM), F32)],
        compiler_params=_cparams(("parallel",)),
        name="rwkv7",
    )(z, vecs, mul, w2, a2, g2)


def _gate_prep_kernel(zs_ref, bias_ref, ct_ref, gr_ref, gc_ref):
    LP = zs_ref.shape[1]
    TB, C, PB = ATT_BLOCK, CHUNK, GATE_PREP_BLOCK
    ct_ref[...] = jnp.zeros_like(ct_ref)
    gr_ref[...] = jnp.zeros_like(gr_ref)
    sel = (_iota((16, LANES), 0) == _iota((16, LANES), 1)).astype(F32)
    sel_m = (_iota((SUBLANES, LANES), 1)
             == _iota((SUBLANES, LANES), 0) + MLSTM_GATE_LANE_I).astype(F32)
    carry = jnp.zeros((16, 1), F32)
    for r0 in range(0, LP, PB):
        n = min(PB, LP - r0)
        kb, off = divmod(r0, TB)
        gz = zs_ref[0, r0:r0 + n, :] + bias_ref[...]
        lf = _log_sigmoid(gz)
        ri = _iota((n, n), 0)
        ci = _iota((n, n), 1)
        cs = _dot(_dot_nt(sel, lf, HI), (ri <= ci).astype(F32), HI) + carry
        carry = cs[:, n - 1:n]
        for hp in range(ct_ref.shape[1]):
            ct_ref[0, hp, kb, 0:2, off:off + n] = cs[2 * hp:2 * hp + 2, :]
        same = ri // C == ci // C
        x = jnp.where(_iota((n, LANES), 1) < MLSTM_GATE_LANE_F, gz, lf)
        xt = _dot_nt(sel_m, x, HI)
        bt = _dot(xt, ((ri <= ci) & same).astype(F32), HI)
        rows = jnp.where(_iota((SUBLANES, n), 0) < MLSTM_HEADS, xt, bt)
        for q in range(n // C):
            gr_ref[0, r0 // C + q, :, 0:C] = rows[:, q * C:(q + 1) * C]
        bc = _dot(((ri >= ci) & same).astype(F32), x, HI)
        gc_ref[0, r0:r0 + n, :] = jnp.where(_iota((n, LANES), 1) < MLSTM_GATE_LANE_F, x, bc)


def _gate_prep(zs, bias, B, LP, n_pairs):
    nkb = pl.cdiv(LP, ATT_BLOCK)
    nchunk = LP // CHUNK
    return pl.pallas_call(
        _gate_prep_kernel,
        grid=(B,),
        in_specs=[pl.BlockSpec((1, LP, LANES), lambda b: (b, 0, 0)),
                  pl.BlockSpec((1, LANES), lambda b: (0, 0))],
        out_specs=[pl.BlockSpec((1, n_pairs, nkb, SUBLANES, ATT_BLOCK), lambda b: (b, 0, 0, 0, 0)),
                   pl.BlockSpec((1, nchunk, SUBLANES, LANES), lambda b: (b, 0, 0, 0)),
                   pl.BlockSpec((1, LP, LANES), lambda b: (b, 0, 0))],
        out_shape=[jax.ShapeDtypeStruct((B, n_pairs, nkb, SUBLANES, ATT_BLOCK), F32),
                   jax.ShapeDtypeStruct((B, nchunk, SUBLANES, LANES), F32),
                   jax.ShapeDtypeStruct((B, LP, LANES), F32)],
        compiler_params=_cparams(("parallel",)),
        name="gate_prep",
    )(zs, bias)


def _fox_kernel(q_ref, k_ref, v_ref, ct_ref, o_ref):
    TB, N = ATT_BLOCK, FOX_HEAD_DIM
    LP = k_ref.shape[1]
    n_full = LP // TB
    tail = LP - n_full * TB
    qi = pl.program_id(2)

    def attend(rows):
        halves = 2 if rows == TB else 1
        HQ = rows // halves
        units = [(h, s) for h in range(2) for s in range(halves)]
        nu = len(units)
        lane_q = _iota((HQ, LANES), 1)
        scale = N ** -0.5
        qu = [jnp.where((lane_q < N) == (h == 0), q_ref[0, s * HQ:(s + 1) * HQ, :] * scale, 0.0)
              for h, s in units]

        def block(k0, n, cks, masks, state):
            kblk = k_ref[0, pl.ds(k0, n), :]
            vblk = v_ref[0, pl.ds(k0, n), :]
            lane_v = _iota((n, LANES), 1)
            vaug = [jnp.where((lane_v < N) == (h == 0), vblk, 1.0) for h in range(2)]
            sc = [_dot_nt(qu[i], kblk) - cks[h] for i, (h, s) in enumerate(units)]
            if masks is not None:
                sc = [jnp.where(masks[s], sc[i], -jnp.inf) for i, (h, s) in enumerate(units)]
            m_new = [jnp.maximum(state[i][0], jnp.max(sc[i], axis=-1, keepdims=True))
                     for i in range(nu)]
            p = [jnp.exp(sc[i] - m_new[i]).astype(BF16) for i in range(nu)]
            pv = [_dot(p[i], vaug[h]) for i, (h, s) in enumerate(units)]
            return tuple((m_new[i], jnp.exp(state[i][0] - m_new[i]) * state[i][1] + pv[i])
                         for i in range(nu))

        ck = lambda kb, h: ct_ref[0, 0, kb, h:h + 1, :]

        def pair_body(kp, state):
            cks = [jnp.concatenate([ck(2 * kp, h), ck(2 * kp + 1, h)], axis=1) for h in range(2)]
            return block(pl.multiple_of(kp * 2 * TB, 2 * TB), 2 * TB, cks, None, state)

        def single_body(_, state):
            kb = qi - 1
            return block(pl.multiple_of(kb * TB, TB), TB, [ck(kb, h) for h in range(2)], None,
                         state)

        state = tuple((jnp.full((HQ, 1), NEG_BIG, F32), jnp.zeros((HQ, LANES), F32))
                      for _ in units)
        state = lax.fori_loop(0, lax.shift_right_logical(qi, 1), pair_body, state)
        state = lax.fori_loop(0, qi & 1, single_body, state)
        cks = [ck(qi, h)[:, 0:rows] for h in range(2)]
        masks = [_iota((HQ, rows), 0) + s * HQ >= _iota((HQ, rows), 1) for s in range(halves)]
        final = block(pl.multiple_of(qi * TB, TB), rows, cks, masks, state)
        for s in range(halves):
            acc0, acc1 = final[s][1], final[halves + s][1]
            out = jnp.where(lane_q < N, acc0 / acc0[:, N:N + 1], acc1 / acc1[:, 0:1])
            o_ref[0, s * HQ:(s + 1) * HQ, :] = out.astype(o_ref.dtype)

    if tail == 0:
        attend(TB)
    else:
        pl.when(qi < n_full)(lambda: attend(TB))
        pl.when(qi == n_full)(lambda: attend(tail))


def _fox(zq, ct, B, LP, width):
    n_pairs = width // LANES
    nq = pl.cdiv(LP, ATT_BLOCK)
    nkb = ct.shape[2]
    kv = lambda off: pl.BlockSpec((1, LP, LANES), lambda b, p, i: (b, 0, off + p))
    return pl.pallas_call(
        _fox_kernel,
        grid=(B, n_pairs, nq),
        in_specs=[pl.BlockSpec((1, ATT_BLOCK, LANES), lambda b, p, i: (b, i, p)),
                  kv(n_pairs), kv(2 * n_pairs),
                  pl.BlockSpec((1, 1, nkb, SUBLANES, ATT_BLOCK), lambda b, p, i: (b, p, 0, 0, 0))],
        out_specs=pl.BlockSpec((1, ATT_BLOCK, LANES), lambda b, p, i: (b, i, p)),
        out_shape=jax.ShapeDtypeStruct((B, LP, width), BF16),
        compiler_params=_cparams(("parallel", "parallel", "arbitrary")),
        name="fox_attention",
    )(zq, zq, zq, ct)


MLSTM_GATE_LANE_I = 12
MLSTM_GATE_LANE_F = 16


def _conv_silu(ref, r0, p0, notfirst, w_ref):
    cur = ref[0, pl.ds(r0, CHUNK), :].astype(F32)
    prev = ref[0, pl.ds(p0, BF16_ROWS), :][BF16_ROWS - SUBLANES:, :].astype(F32) * notfirst
    row8 = _iota(prev.shape, 0)
    y = cur * w_ref[MLSTM_CONV - 1:MLSTM_CONV, :]
    for s in range(1, MLSTM_CONV):
        rolled = pltpu.roll(cur, s, 0)
        top = jnp.where(row8 < s, pltpu.roll(prev, s, 0), rolled[0:SUBLANES])
        shifted = jnp.concatenate([top, rolled[SUBLANES:]], axis=0)
        y = y + shifted * w_ref[MLSTM_CONV - 1 - s:MLSTM_CONV - s, :]
    return y * _sigmoid(y)


def _mlstm_kernel(q_ref, k_ref, v_ref, og_ref, gr_ref, gc_ref, cwq_ref, cwk_ref,
                  o_ref, c_ref, m_ref):
    C, H = CHUNK, MLSTM_HEADS
    Dh = q_ref.shape[2] // H
    nchunk = q_ref.shape[1] // C
    c_ref[...] = jnp.zeros_like(c_ref)
    m_ref[...] = jnp.zeros_like(m_ref)

    incl = _iota((C, C), 0) >= _iota((C, C), 1)
    ones_col = (_iota((C, LANES), 1) == 0).astype(BF16)

    def body(c, carry):
        r0 = pl.multiple_of(c * C, C)
        p0 = pl.multiple_of(jnp.maximum(r0 - BF16_ROWS, 0), BF16_ROWS)
        notfirst = jnp.where(c > 0, 1.0, 0.0).astype(F32)
        q = _conv_silu(q_ref, r0, p0, notfirst, cwq_ref)
        k = _conv_silu(k_ref, r0, p0, notfirst, cwk_ref) * (Dh ** -0.5)
        v = v_ref[0, pl.ds(r0, C), :]
        og = og_ref[0, pl.ds(r0, C), :].astype(F32)
        g_rows = gr_ref[0, c]
        g_cols = gc_ref[0, pl.ds(r0, C), :]

        hs = range(H)
        sl = [slice(h * Dh, (h + 1) * Dh) for h in hs]
        m_all = m_ref[...]
        m_old = [m_all[h:h + 1, 0:1] for h in hs]
        c_old = [c_ref[h] for h in hs]
        li_r = [g_rows[h:h + 1, 0:C] for h in hs]
        b_r = [g_rows[H + h:H + h + 1, 0:C] for h in hs]
        li_c = [g_cols[:, MLSTM_GATE_LANE_I + h:MLSTM_GATE_LANE_I + h + 1] for h in hs]
        b_c = [g_cols[:, MLSTM_GATE_LANE_F + h:MLSTM_GATE_LANE_F + h + 1] for h in hs]
        qh = [q[:, sl[h]].astype(BF16) for h in hs]
        kh = [k[:, sl[h]] for h in hs]
        vaug = [jnp.concatenate([v[:, sl[h]], ones_col], axis=1) for h in hs]
        qk = [_dot_nt(qh[h], kh[h].astype(BF16)) for h in hs]
        qc = [_dot(qh[h], c_old[h].astype(BF16)) for h in hs]
        log_d = [jnp.where(incl, b_c[h] - b_r[h] + li_r[h], -jnp.inf) for h in hs]
        log_inter = [b_c[h] + m_old[h] for h in hs]
        m_q = [jnp.maximum(log_inter[h], jnp.max(log_d[h], axis=-1, keepdims=True)) for h in hs]
        s = [(qk[h] * jnp.exp(log_d[h] - m_q[h])).astype(BF16) for h in hs]
        b_end = [b_c[h][C - 1:C, :] for h in hs]
        m_new = [jnp.maximum(b_end[h] + m_old[h],
                             jnp.max(b_end[h] - b_r[h] + li_r[h], axis=-1, keepdims=True))
                 for h in hs]
        gk = [jnp.exp(b_end[h] - b_c[h] + li_c[h] - m_new[h]) for h in hs]
        num = [jnp.exp(log_inter[h] - m_q[h]) * qc[h] + _dot(s[h], vaug[h]) for h in hs]
        kv = [_dot_tn((gk[h] * kh[h]).astype(BF16), vaug[h]) for h in hs]
        outs = [_sigmoid(og[:, sl[h]]) * num[h][:, :Dh]
                / jnp.maximum(jnp.abs(num[h][:, Dh:Dh + 1]), jnp.exp(-m_q[h])) for h in hs]
        o_ref[0, pl.ds(r0, C), :] = jnp.concatenate(outs, axis=1).astype(o_ref.dtype)
        for h in hs:
            c_ref[h] = jnp.exp(b_end[h] + m_old[h] - m_new[h]) * c_old[h] + kv[h]
        m_ref[0:H, :] = jnp.concatenate([jnp.broadcast_to(m_new[h], (1, LANES)) for h in hs],
                                        axis=0)
        return carry

    lax.fori_loop(0, nchunk, body, 0)


def _mlstm(zm, g_rows, g_cols, cwq, cwk, B, LP):
    W = zm.shape[2] // 4
    Dh = W // MLSTM_HEADS
    seq = lambda j: pl.BlockSpec((1, LP, W), lambda b: (b, 0, j))
    cw = pl.BlockSpec((SUBLANES, W), lambda b: (0, 0))
    return pl.pallas_call(
        _mlstm_kernel,
        grid=(B,),
        in_specs=[seq(0), seq(1), seq(2), seq(3),
                  pl.BlockSpec((1, LP // CHUNK, SUBLANES, LANES), lambda b: (b, 0, 0, 0)),
                  pl.BlockSpec((1, LP, LANES), lambda b: (b, 0, 0)), cw, cw],
        out_specs=pl.BlockSpec((1, LP, W), lambda b: (b, 0, 0)),
        out_shape=jax.ShapeDtypeStruct((B, LP, W), BF16),
        scratch_shapes=[pltpu.VMEM((MLSTM_HEADS, Dh, 2 * Dh), F32),
                        pltpu.VMEM((SUBLANES, LANES), F32)],
        compiler_params=_cparams(("parallel",)),
        name="mlstm",
    )(zm, zm, zm, zm, g_rows, g_cols, cwq, cwk)


def _pad_rows(w, n):
    return jnp.pad(w, ((0, n - w.shape[0]), (0, 0)))


def kernel(x, meta_tokens, ln_emb_g, ln_emb_b, w_in, rwkv_mu, rwkv_w0, rwkv_w2, rwkv_a0, rwkv_a2, rwkv_g2, rwkv_k_k, rwkv_k_a, rwkv_r_k, rwkv_gn_g, rwkv_gn_b, fox_b_f, mlstm_conv_w, mlstm_b_i, mlstm_b_f, proj_rwkv, proj_fox, proj_mlstm, w_out, ln1_g, ln1_b, ffn_w1, ffn_w3, ffn_w2, router_w, moe_w1, moe_w3, moe_w2, ln2_g, ln2_b):
    B, S, D = x.shape
    L = N_META + S
    LP = -(-L // CHUNK) * CHUNK
    T = B * LP
    RW = proj_rwkv.shape[1]
    FW = proj_fox.shape[1]
    MW = proj_mlstm.shape[1]
    n_fox_heads = FW // FOX_HEAD_DIM
    dr, ar, gr = RWKV_DECAY_RANK, RWKV_AAA_RANK, RWKV_GATE_RANK
    c1 = 3 * RW + dr + ar + gr

    meta = jnp.broadcast_to(meta_tokens[None].astype(x.dtype), (B, N_META, D))
    hin = jnp.concatenate([meta, x, jnp.zeros((B, LP - L, D), x.dtype)], axis=1).reshape(T, D)
    h, hb = _ln(hin, ln_emb_g, ln_emb_b)

    ffn_b = tuple(w.astype(BF16) for w in (ffn_w1, ffn_w3, ffn_w2))
    moe_b = tuple(w.astype(BF16) for w in (moe_w1, moe_w3, moe_w2))

    for l in range(DEPTH):
        w_r, w_f, w_m, w_s, w_g = _prep_w_in(w_in, l, RW, FW, MW, dr, ar, gr)

        z_r = _mm(hb, w_r, BF16, w_r.shape[1], "in_rwkv").reshape(B, LP, -1)
        z_f = _mm(hb, w_f, BF16, w_f.shape[1], "in_fox").reshape(B, LP, -1)
        z_m = _mm(hb, w_m, BF16, w_m.shape[1], "in_mlstm").reshape(B, LP, -1)
        z_s = _mm(hb, w_s, F32, LANES, "in_gates").reshape(B, LP, LANES)

        mu = rwkv_mu[l]
        vecs = jnp.zeros((16, RW), F32)
        for i, p in enumerate([rwkv_w0[l], rwkv_a0[l], rwkv_k_k[l], rwkv_k_a[l],
                               rwkv_r_k[l].reshape(RW), rwkv_gn_g[l], rwkv_gn_b[l]]):
            vecs = vecs.at[i].set(p)
        vecs = vecs.at[8].set(mu[:RW]).at[9].set(mu[RW:2 * RW]).at[10].set(mu[2 * RW:3 * RW])
        mul = jnp.zeros((8, RWKV_LORA_W), F32)
        mul = mul.at[0, 0:dr].set(mu[3 * RW:3 * RW + dr])
        mul = mul.at[0, LANES:LANES + ar].set(mu[3 * RW + dr:3 * RW + dr + ar])
        mul = mul.at[0, 2 * LANES:2 * LANES + gr].set(mu[3 * RW + dr + ar:c1])
        y_r = _rwkv(z_r, vecs, mul, _pad_rows(rwkv_w2[l], LANES).astype(BF16),
                    _pad_rows(rwkv_a2[l], LANES).astype(BF16), rwkv_g2[l].astype(BF16), B, LP)

        bias = jnp.zeros((1, LANES), F32)
        bias = bias.at[0, 0:n_fox_heads].set(fox_b_f[l])
        bias = bias.at[0, MLSTM_GATE_LANE_I:MLSTM_GATE_LANE_I + MLSTM_HEADS].set(mlstm_b_i[l])
        bias = bias.at[0, MLSTM_GATE_LANE_F:MLSTM_GATE_LANE_F + MLSTM_HEADS].set(mlstm_b_f[l])

        ct, g_rows, g_cols = _gate_prep(z_s, bias, B, LP, FW // LANES)
        y_f = _fox(z_f, ct, B, LP, FW)

        cw = _pad_rows(mlstm_conv_w[l], SUBLANES)
        y_m = _mlstm(z_m, g_rows, g_cols, cw[:, :MW], cw[:, MW:], B, LP)

        pre = _merge(hb, w_g, y_r.reshape(T, RW), y_f.reshape(T, FW), y_m.reshape(T, MW),
                     proj_rwkv[l].astype(BF16), proj_fox[l].astype(BF16),
                     proj_mlstm[l].astype(BF16))
        h, hb = _mm_res_ln(pre, w_out[l].astype(BF16), h, ln1_g[l], ln1_b[l])

        i = l // 2
        if l % 2 == 0:
            h, hb = _ffn(hb, ffn_b, i, h, ln2_g[l], ln2_b[l])
        else:
            tm = _tile(T, MOE_TILE)
            n_tiles = 2 * T // tm + N_EXPERTS
            gw, ids = _router(h, router_w[i])
            row_token, dest, tile_expert, n_used = _route(ids, tm, n_tiles)
            xs = _gather_rows(h, row_token)
            ys = _moe_ffn(xs, moe_b, i, tile_expert, n_used, tm)
            h, hb = _combine(ys, dest, gw, h, ln2_g[l], ln2_b[l])

    return h.reshape(B, LP, D)[:, N_META:L]
```

```python
import functools
import math

import jax
import jax.numpy as jnp
from jax import lax
from jax.experimental import pallas as pl
from jax.experimental.pallas import tpu as pltpu

F32 = jnp.float32
BF16 = jnp.bfloat16
HI = lax.Precision.HIGHEST

N_META = 16
DEPTH = 4
RWKV_HEAD_DIM = 64
RWKV_DECAY_RANK = 96
RWKV_AAA_RANK = 96
RWKV_GATE_RANK = 256
RWKV_GN_EPS = 64e-5
FOX_HEAD_DIM = 64
MLSTM_HEADS = 4
MLSTM_CONV = 4
N_EXPERTS = 8
LN_EPS = 1e-5
DEEPNORM_ALPHA = (2 * DEPTH) ** 0.25

CHUNK = 64
LANES = 128
SUBLANES = 8
BF16_ROWS = 16
ATT_BLOCK = 512
GATE_PREP_BLOCK = 256
FFN_ROWS = 512
MOE_TILE = 1024
DMA_UNROLL = 8
VMEM_LIMIT = 56 * 1024 * 1024
NEG_BIG = -1e30


def _cparams(sem):
    return pltpu.CompilerParams(dimension_semantics=sem, vmem_limit_bytes=VMEM_LIMIT)


def _tile(n, pref, mult=SUBLANES):
    best = None
    for d in range(mult, min(n, pref) + 1, mult):
        if n % d == 0:
            best = d
    return best if best is not None else n


def _dot(a, b, precision=None):
    return jnp.dot(a, b, preferred_element_type=F32, precision=precision)


def _dot_nt(a, b, precision=None):
    return lax.dot_general(a, b, (((1,), (1,)), ((), ())), preferred_element_type=F32,
                           precision=precision)


def _dot_tn(a, b, precision=None):
    return lax.dot_general(a, b, (((0,), (0,)), ((), ())), preferred_element_type=F32,
                           precision=precision)


def _softplus(x):
    return jnp.maximum(x, 0.0) + jnp.log(1.0 + jnp.exp(-jnp.abs(x)))


def _log_sigmoid(x):
    return jnp.minimum(x, 0.0) - jnp.log(1.0 + jnp.exp(-jnp.abs(x)))


def _sigmoid(x):
    return 1.0 / (1.0 + jnp.exp(-x))


def _iota(shape, dim):
    return lax.broadcasted_iota(jnp.int32, shape, dim)


def _layer_norm(t, g, b):
    mu = jnp.mean(t, axis=-1, keepdims=True)
    d = t - mu
    var = jnp.mean(d * d, axis=-1, keepdims=True)
    return d * lax.rsqrt(var + LN_EPS) * g + b


def _ln_kernel(x_ref, g_ref, b_ref, o_ref, ob_ref):
    y = _layer_norm(x_ref[...], g_ref[...], b_ref[...])
    o_ref[...] = y
    ob_ref[...] = y.astype(BF16)


def _ln(x, g, b):
    T, D = x.shape
    bm = _tile(T, 512)
    return pl.pallas_call(
        _ln_kernel,
        grid=(T // bm,),
        in_specs=[pl.BlockSpec((bm, D), lambda i: (i, 0)),
                  pl.BlockSpec((1, D), lambda i: (0, 0)),
                  pl.BlockSpec((1, D), lambda i: (0, 0))],
        out_specs=[pl.BlockSpec((bm, D), lambda i: (i, 0)),
                   pl.BlockSpec((bm, D), lambda i: (i, 0))],
        out_shape=[jax.ShapeDtypeStruct((T, D), F32), jax.ShapeDtypeStruct((T, D), BF16)],
        compiler_params=_cparams(("parallel",)),
        name="ln_embed",
    )(x, g.reshape(1, D), b.reshape(1, D))


def _prep_w_in_kernel(w_ref, or_ref, of_ref, om_ref, os_ref, og_ref, *, rw, fw, mw, dr, ar):
    x = w_ref[0]
    c1 = or_ref.shape[1] - 2 * LANES + dr + ar
    c2 = c1 + 3 * fw + fw // FOX_HEAD_DIM
    c3 = c2 + 4 * mw + 2 * MLSTM_HEADS
    zeros = lambda n: jnp.zeros((x.shape[0], n), F32)
    cast = lambda parts: jnp.concatenate(parts, axis=1).astype(BF16)
    or_ref[...] = cast([x[:, 3 * rw:3 * rw + dr], zeros(LANES - dr),
                        x[:, 3 * rw + dr:3 * rw + dr + ar], zeros(LANES - ar),
                        x[:, 3 * rw + dr + ar:c1], x[:, 0:3 * rw]])
    of_ref[...] = x[:, c1:c1 + 3 * fw].astype(BF16)
    om_ref[...] = x[:, c2:c2 + 4 * mw].astype(BF16)
    n_small = c2 - (c1 + 3 * fw) + c3 - (c2 + 4 * mw)
    os_ref[...] = cast([x[:, c1 + 3 * fw:c2], x[:, c2 + 4 * mw:c3], zeros(LANES - n_small)])
    og_ref[...] = x[:, c3:].astype(BF16)


def _prep_w_in(w_in, l, rw, fw, mw, dr, ar, gr):
    _, D, n_in = w_in.shape
    rows = _tile(D, 128, BF16_ROWS)
    widths = [3 * rw + 2 * LANES + gr, 3 * fw, 4 * mw, LANES, 3 * D]
    return pl.pallas_call(
        functools.partial(_prep_w_in_kernel, rw=rw, fw=fw, mw=mw, dr=dr, ar=ar),
        grid=(D // rows,),
        in_specs=[pl.BlockSpec((1, rows, n_in), lambda i: (l, i, 0))],
        out_specs=[pl.BlockSpec((rows, w), lambda i: (i, 0)) for w in widths],
        out_shape=[jax.ShapeDtypeStruct((D, w), BF16) for w in widths],
        compiler_params=_cparams(("parallel",)),
        name="prep_w_in",
    )(w_in)


def _mm_kernel(a_ref, b_ref, o_ref):
    o_ref[...] = _dot(a_ref[...], b_ref[...]).astype(o_ref.dtype)


def _mm(a, b, out_dtype, bn_pref, name):
    M, K = a.shape
    N = b.shape[1]
    bm = _tile(M, 1024)
    bn = _tile(N, bn_pref, LANES)
    return pl.pallas_call(
        _mm_kernel,
        grid=(M // bm, N // bn),
        in_specs=[pl.BlockSpec((bm, K), lambda i, j: (i, 0)),
                  pl.BlockSpec((K, bn), lambda i, j: (0, j))],
        out_specs=pl.BlockSpec((bm, bn), lambda i, j: (i, j)),
        out_shape=jax.ShapeDtypeStruct((M, N), out_dtype),
        compiler_params=_cparams(("parallel", "arbitrary")),
        name=name,
    )(a, b)


def _mm_res_ln_kernel(a_ref, w_ref, h_ref, g_ref, b_ref, o_ref, ob_ref):
    t = DEEPNORM_ALPHA * h_ref[...] + _dot(a_ref[...], w_ref[...])
    y = _layer_norm(t, g_ref[...], b_ref[...])
    o_ref[...] = y
    ob_ref[...] = y.astype(BF16)


def _mm_res_ln(a, w, h, g, b):
    M, K = a.shape
    D = w.shape[1]
    bm = _tile(M, 512)
    return pl.pallas_call(
        _mm_res_ln_kernel,
        grid=(M // bm,),
        in_specs=[pl.BlockSpec((bm, K), lambda i: (i, 0)),
                  pl.BlockSpec((K, D), lambda i: (0, 0)),
                  pl.BlockSpec((bm, D), lambda i: (i, 0)),
                  pl.BlockSpec((1, D), lambda i: (0, 0)),
                  pl.BlockSpec((1, D), lambda i: (0, 0))],
        out_specs=[pl.BlockSpec((bm, D), lambda i: (i, 0)),
                   pl.BlockSpec((bm, D), lambda i: (i, 0))],
        out_shape=[jax.ShapeDtypeStruct((M, D), F32), jax.ShapeDtypeStruct((M, D), BF16)],
        compiler_params=_cparams(("parallel",)),
        name="out_proj_ln",
    )(a, w, h, g.reshape(1, D), b.reshape(1, D))


def _merge_kernel(hb_ref, wgr_ref, wgf_ref, wgm_ref, yr_ref, yf_ref, ym_ref, pr_ref, pf_ref,
                  pm_ref, o_ref):
    hb = hb_ref[...]
    acc = _sigmoid(_dot(hb, wgr_ref[...])) * _dot(yr_ref[...], pr_ref[...])
    acc += _sigmoid(_dot(hb, wgf_ref[...])) * _dot(yf_ref[...], pf_ref[...])
    acc += _sigmoid(_dot(hb, wgm_ref[...])) * _dot(ym_ref[...], pm_ref[...])
    o_ref[...] = acc.astype(o_ref.dtype)


def _merge(hb, wg, yr, yf, ym, pr, pf, pm):
    M, D = hb.shape
    bm = _tile(M, 1024)
    bn = _tile(D, 512, LANES)
    nb = D // bn
    row = lambda w: pl.BlockSpec((bm, w), lambda i, j: (i, 0))
    col = lambda k: pl.BlockSpec((k, bn), lambda i, j: (0, j))
    gate = lambda br: pl.BlockSpec((D, bn), lambda i, j: (0, br * nb + j))
    return pl.pallas_call(
        _merge_kernel,
        grid=(M // bm, nb),
        in_specs=[row(D), gate(0), gate(1), gate(2),
                  row(yr.shape[1]), row(yf.shape[1]), row(ym.shape[1]),
                  col(pr.shape[0]), col(pf.shape[0]), col(pm.shape[0])],
        out_specs=pl.BlockSpec((bm, bn), lambda i, j: (i, j)),
        out_shape=jax.ShapeDtypeStruct((M, D), BF16),
        compiler_params=_cparams(("parallel", "arbitrary")),
        name="merge",
    )(hb, wg, wg, wg, yr, yf, ym, pr, pf, pm)


def _router_kernel(h_ref, w_ref, ow_ref, oi_ref):
    logits = _dot(h_ref[...], w_ref[...], HI)
    lane = _iota(logits.shape, 1)
    logits = jnp.where(lane < N_EXPERTS, logits, -jnp.inf)
    m1 = jnp.max(logits, axis=-1, keepdims=True)
    i1 = jnp.min(jnp.where(logits == m1, lane, LANES), axis=-1, keepdims=True)
    rest = jnp.where(lane == i1, -jnp.inf, logits)
    m2 = jnp.max(rest, axis=-1, keepdims=True)
    i2 = jnp.min(jnp.where(rest == m2, lane, LANES), axis=-1, keepdims=True)
    e2 = jnp.exp(m2 - m1)
    den = 1.0 + e2
    ow_ref[...] = jnp.where(lane == 0, 1.0 / den, jnp.where(lane == 1, e2 / den, 0.0))
    oi_ref[...] = jnp.where(lane == 0, i1, jnp.where(lane == 1, i2, 0))


def _router(h, w):
    T, D = h.shape
    bm = _tile(T, 512)
    wp = jnp.zeros((D, LANES), F32).at[:, :N_EXPERTS].set(w)
    out = pl.BlockSpec((bm, LANES), lambda i: (i, 0))
    return pl.pallas_call(
        _router_kernel,
        grid=(T // bm,),
        in_specs=[pl.BlockSpec((bm, D), lambda i: (i, 0)),
                  pl.BlockSpec((D, LANES), lambda i: (0, 0))],
        out_specs=[out, out],
        out_shape=[jax.ShapeDtypeStruct((T, LANES), F32),
                   jax.ShapeDtypeStruct((T, LANES), jnp.int32)],
        compiler_params=_cparams(("parallel",)),
        name="router",
    )(h, wp)


def _route(ids, tm, n_tiles):
    T = ids.shape[0]
    e = ids[:, :2].reshape(-1)
    onehot = (e[:, None] == jnp.arange(N_EXPERTS, dtype=jnp.int32)[None, :]).astype(jnp.int32)
    pos = jnp.take_along_axis(jnp.cumsum(onehot, axis=0), e[:, None], axis=1)[:, 0] - 1
    counts = jnp.sum(onehot, axis=0)
    padded = (counts + tm - 1) // tm * tm
    ends = jnp.cumsum(padded)
    dest = (ends - padded)[e] + pos
    row_token = jnp.zeros((n_tiles * tm,), jnp.int32).at[dest].set(
        jnp.arange(2 * T, dtype=jnp.int32) // 2)
    tile_start = jnp.arange(n_tiles, dtype=jnp.int32) * tm
    tile_expert = jnp.minimum(
        jnp.sum((ends[None, :] <= tile_start[:, None]).astype(jnp.int32), axis=1), N_EXPERTS - 1)
    n_used = (ends[-1] // tm).astype(jnp.int32).reshape(1)
    return row_token.reshape(n_tiles, 1, tm), dest, tile_expert, n_used


def _row_copy(src_ref, buf_ref, sem, src_row, dst_row):
    return pltpu.make_async_copy(src_ref.at[pl.ds(src_row, 1), :],
                                 buf_ref.at[pl.ds(dst_row, 1), :], sem.at[0])


def _gather_kernel(idx_ref, nu_ref, src_ref, o_ref, buf_ref, sem):
    tm = buf_ref.shape[0]
    used = pl.program_id(0) < nu_ref[0]

    def issue(r, c):
        _row_copy(src_ref, buf_ref, sem, idx_ref[0, 0, r], r).start()
        return c

    def drain(r, c):
        _row_copy(src_ref, buf_ref, sem, 0, r).wait()
        return c

    @pl.when(used)
    def _():
        lax.fori_loop(0, tm, issue, 0, unroll=DMA_UNROLL)
        lax.fori_loop(0, tm, drain, 0, unroll=DMA_UNROLL)
        o_ref[...] = buf_ref[...].astype(o_ref.dtype)

    @pl.when(jnp.logical_not(used))
    def _():
        o_ref[...] = jnp.zeros_like(o_ref)


def _gather_rows(src, row_token, n_used):
    n_tiles, _, tm = row_token.shape
    D = src.shape[1]
    return pl.pallas_call(
        _gather_kernel,
        grid=(n_tiles,),
        in_specs=[pl.BlockSpec((1, 1, tm), lambda i: (i, 0, 0), memory_space=pltpu.SMEM),
                  pl.BlockSpec(memory_space=pltpu.SMEM),
                  pl.BlockSpec(memory_space=pl.ANY)],
        out_specs=pl.BlockSpec((tm, D), lambda i: (i, 0)),
        out_shape=jax.ShapeDtypeStruct((n_tiles * tm, D), BF16),
        scratch_shapes=[pltpu.VMEM((tm, D), F32), pltpu.SemaphoreType.DMA((1,))],
        compiler_params=_cparams(("arbitrary",)),
        name="moe_gather",
    )(row_token, n_used, src)


def _moe_ffn_kernel(te_ref, nu_ref, x_ref, w1_ref, w3_ref, w2_ref, o_ref):
    j = pl.program_id(1)

    @pl.when(j == 0)
    def _():
        o_ref[...] = jnp.zeros_like(o_ref)

    @pl.when(pl.program_id(0) < nu_ref[0])
    def _():
        _swiglu_accumulate(x_ref, w1_ref[0, 0], w3_ref[0, 0], w2_ref[0, 0], o_ref)


def _moe_ffn(xs, weights, layer, tile_expert, n_used, tm):
    w1, w3, w2 = weights
    NP, D = xs.shape
    FF = w1.shape[3]
    bf = _tile(FF, 512, LANES)
    grid_spec = pltpu.PrefetchScalarGridSpec(
        num_scalar_prefetch=2,
        grid=(NP // tm, FF // bf),
        in_specs=[pl.BlockSpec((tm, D), lambda i, j, te, nu: (i, 0)),
                  pl.BlockSpec((1, 1, D, bf),
                               lambda i, j, te, nu: (layer, te[i], 0, jnp.where(i < nu[0], j, 0))),
                  pl.BlockSpec((1, 1, D, bf),
                               lambda i, j, te, nu: (layer, te[i], 0, jnp.where(i < nu[0], j, 0))),
                  pl.BlockSpec((1, 1, bf, D),
                               lambda i, j, te, nu: (layer, te[i], jnp.where(i < nu[0], j, 0), 0))],
        out_specs=pl.BlockSpec((tm, D), lambda i, j, te, nu: (i, 0)),
    )
    return pl.pallas_call(
        _moe_ffn_kernel,
        grid_spec=grid_spec,
        out_shape=jax.ShapeDtypeStruct((NP, D), F32),
        compiler_params=_cparams(("parallel", "arbitrary")),
        name="moe_ffn",
    )(tile_expert, n_used, xs, w1, w3, w2)


def _combine_kernel(d_ref, y_ref, gw_ref, h_ref, g_ref, b_ref, o_ref, ob_ref, buf_ref, sem):
    bm = buf_ref.shape[1]

    def issue(g, c):
        for u in range(DMA_UNROLL // 2):
            t = g * (DMA_UNROLL // 2) + u
            for r in range(2):
                _row_copy(y_ref, buf_ref.at[r], sem, d_ref[0, 0, 2 * t + r], t).start(priority=r)
        return c

    def drain(t, c):
        for r in range(2):
            _row_copy(y_ref, buf_ref.at[r], sem, 0, t).wait()
        return c

    lax.fori_loop(0, bm // (DMA_UNROLL // 2), issue, 0)
    lax.fori_loop(0, bm, drain, 0, unroll=DMA_UNROLL // 2)
    gw = gw_ref[...]
    ff = gw[:, 0:1] * buf_ref[0] + gw[:, 1:2] * buf_ref[1]
    y = _layer_norm(DEEPNORM_ALPHA * h_ref[...] + ff, g_ref[...], b_ref[...])
    o_ref[...] = y
    ob_ref[...] = y.astype(BF16)


def _combine(ys, dest, gw, h, g, b):
    T, D = h.shape
    bm = _tile(T, 256)
    row = lambda w: pl.BlockSpec((bm, w), lambda i: (i, 0))
    vec = pl.BlockSpec((1, D), lambda i: (0, 0))
    return pl.pallas_call(
        _combine_kernel,
        grid=(T // bm,),
        in_specs=[pl.BlockSpec((1, 1, 2 * bm), lambda i: (i, 0, 0), memory_space=pltpu.SMEM),
                  pl.BlockSpec(memory_space=pl.ANY), row(LANES), row(D), vec, vec],
        out_specs=[row(D), row(D)],
        out_shape=[jax.ShapeDtypeStruct((T, D), F32), jax.ShapeDtypeStruct((T, D), BF16)],
        scratch_shapes=[pltpu.VMEM((2, bm, D), F32), pltpu.SemaphoreType.DMA((1,))],
        compiler_params=_cparams(("arbitrary",)),
        name="moe_combine",
    )(dest.reshape(T // bm, 1, 2 * bm), ys, gw, h, g.reshape(1, D), b.reshape(1, D))


def _swiglu_accumulate(x_ref, w1, w3, w2, acc_ref):
    hm = x_ref.shape[0] // 2
    rows = [slice(0, hm), slice(hm, 2 * hm)]
    ac = [(_dot(x_ref[r, :], w1), _dot(x_ref[r, :], w3)) for r in rows]
    hh = [(a * _sigmoid(a) * c).astype(BF16) for a, c in ac]
    for r, t in zip(rows, hh):
        acc_ref[r, :] += _dot(t, w2)


def _ffn_kernel(x_ref, w1_ref, w3_ref, w2_ref, h_ref, g_ref, b_ref, o_ref, ob_ref, acc_ref):
    j = pl.program_id(1)

    @pl.when(j == 0)
    def _():
        acc_ref[...] = jnp.zeros_like(acc_ref)

    _swiglu_accumulate(x_ref, w1_ref[0], w3_ref[0], w2_ref[0], acc_ref)

    @pl.when(j == pl.num_programs(1) - 1)
    def _():
        y = _layer_norm(DEEPNORM_ALPHA * h_ref[...] + acc_ref[...], g_ref[...], b_ref[...])
        o_ref[...] = y
        ob_ref[...] = y.astype(BF16)


def _ffn(xb, weights, layer, h, g, b):
    w1, w3, w2 = weights
    M, D = xb.shape
    FF = w1.shape[2]
    bm = _tile(M, FFN_ROWS, 2 * BF16_ROWS)
    bf = _tile(FF, 512, LANES)
    row = lambda w: pl.BlockSpec((bm, w), lambda i, j: (i, 0))
    vec = pl.BlockSpec((1, D), lambda i, j: (0, 0))
    return pl.pallas_call(
        _ffn_kernel,
        grid=(M // bm, FF // bf),
        in_specs=[row(D),
                  pl.BlockSpec((1, D, bf), lambda i, j: (layer, 0, j)),
                  pl.BlockSpec((1, D, bf), lambda i, j: (layer, 0, j)),
                  pl.BlockSpec((1, bf, D), lambda i, j: (layer, j, 0)),
                  row(D), vec, vec],
        out_specs=[row(D), row(D)],
        out_shape=[jax.ShapeDtypeStruct((M, D), F32), jax.ShapeDtypeStruct((M, D), BF16)],
        scratch_shapes=[pltpu.VMEM((bm, D), F32)],
        compiler_params=_cparams(("parallel", "arbitrary")),
        name="ffn",
    )(xb, w1, w3, w2, h, g.reshape(1, D), b.reshape(1, D))


RWKV_LORA_W = 512
RWKV_SUM_W = 256


def _shift_lerp(ref, c0, width, r0, p0, notfirst, mu, row0):
    cur = ref[0, pl.ds(r0, CHUNK), c0:c0 + width].astype(F32)
    prev = ref[0, pl.ds(p0, BF16_ROWS), c0:c0 + width][BF16_ROWS - 1:BF16_ROWS, :].astype(F32)
    zprev = jnp.where(row0, prev * notfirst, pltpu.roll(cur, 1, 0))
    return cur + mu * (zprev - cur)


def _split3(x):
    hi = x.astype(BF16)
    r1 = x - hi.astype(F32)
    mid = r1.astype(BF16)
    lo = (r1 - mid.astype(F32)).astype(BF16)
    return hi, mid, lo


def _rwkv_kernel(z_ref, vec_ref, mul_ref, w2_ref, a2_ref, g2_ref, o_ref, s_ref):
    C, N = CHUNK, RWKV_HEAD_DIM
    W = o_ref.shape[2]
    H = W // N
    LW = RWKV_LORA_W
    nchunk = z_ref.shape[1] // C
    s_ref[...] = jnp.zeros_like(s_ref)

    ri = _iota((C, C), 0)
    ci = _iota((C, C), 1)
    strict = ri > ci
    incl = ri >= ci
    tri = incl.astype(BF16)
    eye = (ri == ci).astype(F32)
    SW = RWKV_SUM_W
    bd = (_iota((SW, SW), 0) // N == _iota((SW, SW), 1) // N).astype(BF16)
    row0_w = _iota((C, W), 0) == 0
    row0_l = _iota((C, LW), 0) == 0

    def head_sum(x):
        xb = x.astype(BF16)
        return jnp.concatenate([_dot(xb[:, g:g + SW], bd) for g in range(0, W, SW)], axis=1)

    w0 = vec_ref[0:1, :]
    a0 = vec_ref[1:2, :]
    k_k = vec_ref[2:3, :]
    k_a = vec_ref[3:4, :]
    r_k = vec_ref[4:5, :]
    gn_g = vec_ref[5:6, :]
    gn_b = vec_ref[6:7, :]
    mu_r = vec_ref[8:9, :]
    mu_k = vec_ref[9:10, :]
    mu_v = vec_ref[10:11, :]
    mu_l = mul_ref[0:1, :]

    def body(c, carry):
        r0 = pl.multiple_of(c * C, C)
        p0 = pl.multiple_of(jnp.maximum(r0 - BF16_ROWS, 0), BF16_ROWS)
        notfirst = jnp.where(c > 0, 1.0, 0.0).astype(F32)
        lo = _shift_lerp(z_ref, 0, LW, r0, p0, notfirst, mu_l, row0_l)
        r = _shift_lerp(z_ref, LW, W, r0, p0, notfirst, mu_r, row0_w)
        k = _shift_lerp(z_ref, LW + W, W, r0, p0, notfirst, mu_k, row0_w)
        v = _shift_lerp(z_ref, LW + 2 * W, W, r0, p0, notfirst, mu_v, row0_w)

        wd = lo[:, 0:128]
        ad = lo[:, 128:256]
        gd = lo[:, 256:512]
        w_log = -_softplus(-(w0 + _dot(jnp.tanh(wd).astype(BF16), w2_ref[...]))) - 0.5
        lw = -jnp.exp(w_log)
        a = _sigmoid(a0 + _dot(ad.astype(BF16), a2_ref[...]))
        g = _dot(_sigmoid(gd).astype(BF16), g2_ref[...])

        kk = k * k_k
        kk = kk / jnp.maximum(jnp.sqrt(head_sum(kk * kk)), 1e-12)
        kmod = k * (1.0 + (a - 1.0) * k_a)
        b = kk * a

        cum = sum(_dot(tri, part) for part in _split3(lw))
        cum_c = cum[C - 1:C, :]
        e_inv = jnp.exp(-cum)
        e_fin = jnp.exp(cum_c - cum)
        kkt = kk * jnp.exp(cum - lw)
        rt = r * jnp.exp(cum)
        bt = b * e_inv
        kt = kmod * e_inv
        nbh = -b * e_fin
        kh = kmod * e_fin
        pc = jnp.exp(cum_c)

        hs = range(H)
        sl = [slice(h * N, (h + 1) * N) for h in hs]
        cat = lambda top, bot, h: jnp.concatenate([top[:, sl[h]], bot[:, sl[h]]], axis=0).astype(BF16)
        lhs = [cat(kkt, rt, h) for h in hs]
        rhs = [cat(bt, kt, h) for h in hs]
        vb = [v[:, sl[h]].astype(BF16) for h in hs]
        s_old = [s_ref[h] for h in hs]
        mall = [_dot_nt(lhs[h], rhs[h]) for h in hs]
        gm = [_dot_nt(lhs[h], s_old[h].astype(BF16)) for h in hs]
        mb = [jnp.where(incl, mall[h][C:, :C], 0.0).astype(BF16) for h in hs]
        akmk = [jnp.concatenate([jnp.where(strict, mall[h][:C, C:], 0.0),
                                 jnp.where(incl, mall[h][C:, C:], 0.0)], axis=0).astype(BF16)
                for h in hs]
        av = [_dot(akmk[h], vb[h]) for h in hs]
        square = lambda xs: [_dot(xs[h], xs[h]).astype(BF16) for h in hs]
        x0 = [jnp.where(strict, -mall[h][:C, :C], 0.0) for h in hs]
        t = [eye + x0[h] for h in hs]
        xp = square([x0[h].astype(BF16) for h in hs])
        for level in range(5):
            xn = square(xp) if level < 4 else None
            t = [t[h] + _dot(t[h].astype(BF16), xp[h]) for h in hs]
            xp = xn
        u = [_dot(t[h].astype(BF16), (gm[h][:C] + av[h][:C]).astype(BF16)) for h in hs]
        o = [gm[h][C:] + av[h][C:] - _dot(mb[h], u[h].astype(BF16)) for h in hs]
        upd = [_dot_tn(jnp.concatenate([u[h].astype(BF16), vb[h]], axis=0), cat(nbh, kh, h))
               for h in hs]
        for h in hs:
            s_ref[h] = s_old[h] * pc[:, sl[h]] + upd[h]

        o = jnp.concatenate(o, axis=1)
        d = o - head_sum(o) * (1.0 / N)
        var = head_sum(d * d) * (1.0 / N)
        on = d * lax.rsqrt(var + RWKV_GN_EPS) * gn_g + gn_b
        bonus = head_sum(r * kmod * r_k) * v
        o_ref[0, pl.ds(r0, C), :] = ((on + bonus) * g).astype(o_ref.dtype)
        return carry

    lax.fori_loop(0, nchunk, body, 0)


def _rwkv(z, vecs, mul, w2, a2, g2, B, LP):
    W = vecs.shape[1]
    ZW = z.shape[2]
    whole = lambda shape: pl.BlockSpec(shape, lambda b: (0,) * len(shape))
    return pl.pallas_call(
        _rwkv_kernel,
        grid=(B,),
        in_specs=[pl.BlockSpec((1, LP, ZW), lambda b: (b, 0, 0)),
                  whole((16, W)), whole((8, RWKV_LORA_W)), whole((LANES, W)), whole((LANES, W)),
                  whole((RWKV_GATE_RANK, W))],
        out_specs=pl.BlockSpec((1, LP, W), lambda b: (b, 0, 0)),
        out_shape=jax.ShapeDtypeStruct((B, LP, W), BF16),
        scratch_shapes=[pltpu.VMEM((W // RWKV_HEAD_DIM, RWKV_HEAD_DIM, RWKV_HEAD_DIM), F32)],
        compiler_params=_cparams(("parallel",)),
        name="rwkv7",
    )(z, vecs, mul, w2, a2, g2)


def _gate_prep_kernel(zs_ref, bias_ref, ct_ref, gr_ref, gc_ref):
    LP = zs_ref.shape[1]
    TB, C, PB = ATT_BLOCK, CHUNK, GATE_PREP_BLOCK
    ct_ref[...] = jnp.zeros_like(ct_ref)
    gr_ref[...] = jnp.zeros_like(gr_ref)
    sel = (_iota((16, LANES), 0) == _iota((16, LANES), 1)).astype(F32)
    sel_m = (_iota((SUBLANES, LANES), 1)
             == _iota((SUBLANES, LANES), 0) + MLSTM_GATE_LANE_I).astype(F32)
    carry = jnp.zeros((16, 1), F32)
    for r0 in range(0, LP, PB):
        n = min(PB, LP - r0)
        kb, off = divmod(r0, TB)
        gz = zs_ref[0, r0:r0 + n, :] + bias_ref[...]
        lf = _log_sigmoid(gz)
        ri = _iota((n, n), 0)
        ci = _iota((n, n), 1)
        cs = _dot(_dot_nt(sel, lf, HI), (ri <= ci).astype(F32), HI) + carry
        carry = cs[:, n - 1:n]
        for hp in range(ct_ref.shape[1]):
            ct_ref[0, hp, kb, 0:2, off:off + n] = cs[2 * hp:2 * hp + 2, :]
        same = ri // C == ci // C
        x = jnp.where(_iota((n, LANES), 1) < MLSTM_GATE_LANE_F, gz, lf)
        xt = _dot_nt(sel_m, x, HI)
        bt = _dot(xt, ((ri <= ci) & same).astype(F32), HI)
        rows = jnp.where(_iota((SUBLANES, n), 0) < MLSTM_HEADS, xt, bt)
        for q in range(n // C):
            gr_ref[0, r0 // C + q, :, 0:C] = rows[:, q * C:(q + 1) * C]
        bc = _dot(((ri >= ci) & same).astype(F32), x, HI)
        gc_ref[0, r0:r0 + n, :] = jnp.where(_iota((n, LANES), 1) < MLSTM_GATE_LANE_F, x, bc)


def _gate_prep(zs, bias, B, LP, n_pairs):
    nkb = pl.cdiv(LP, ATT_BLOCK)
    nchunk = LP // CHUNK
    return pl.pallas_call(
        _gate_prep_kernel,
        grid=(B,),
        in_specs=[pl.BlockSpec((1, LP, LANES), lambda b: (b, 0, 0)),
                  pl.BlockSpec((1, LANES), lambda b: (0, 0))],
        out_specs=[pl.BlockSpec((1, n_pairs, nkb, SUBLANES, ATT_BLOCK), lambda b: (b, 0, 0, 0, 0)),
                   pl.BlockSpec((1, nchunk, SUBLANES, LANES), lambda b: (b, 0, 0, 0)),
                   pl.BlockSpec((1, LP, LANES), lambda b: (b, 0, 0))],
        out_shape=[jax.ShapeDtypeStruct((B, n_pairs, nkb, SUBLANES, ATT_BLOCK), F32),
                   jax.ShapeDtypeStruct((B, nchunk, SUBLANES, LANES), F32),
                   jax.ShapeDtypeStruct((B, LP, LANES), F32)],
        compiler_params=_cparams(("parallel",)),
        name="gate_prep",
    )(zs, bias)


def _fox_kernel(q_ref, k_ref, v_ref, ct_ref, o_ref):
    TB, N = ATT_BLOCK, FOX_HEAD_DIM
    LP = k_ref.shape[1]
    n_full = LP // TB
    tail = LP - n_full * TB
    qi = pl.program_id(2)

    def attend(rows):
        halves = 2 if rows == TB else 1
        HQ = rows // halves
        units = [(h, s) for h in range(2) for s in range(halves)]
        nu = len(units)
        lane_q = _iota((HQ, LANES), 1)
        scale = N ** -0.5
        qu = [jnp.where((lane_q < N) == (h == 0), q_ref[0, s * HQ:(s + 1) * HQ, :] * scale, 0.0)
              for h, s in units]

        def block(k0, n, cks, masks, state):
            kblk = k_ref[0, pl.ds(k0, n), :]
            vblk = v_ref[0, pl.ds(k0, n), :]
            lane_v = _iota((n, LANES), 1)
            vaug = [jnp.where((lane_v < N) == (h == 0), vblk, 1.0) for h in range(2)]
            sc = [_dot_nt(qu[i], kblk) - cks[h] for i, (h, s) in enumerate(units)]
            if masks is not None:
                sc = [jnp.where(masks[s], sc[i], -jnp.inf) for i, (h, s) in enumerate(units)]
            m_new = [jnp.maximum(state[i][0], jnp.max(sc[i], axis=-1, keepdims=True))
                     for i in range(nu)]
            p = [jnp.exp(sc[i] - m_new[i]).astype(BF16) for i in range(nu)]
            pv = [_dot(p[i], vaug[h]) for i, (h, s) in enumerate(units)]
            return tuple((m_new[i], jnp.exp(state[i][0] - m_new[i]) * state[i][1] + pv[i])
                         for i in range(nu))

        ck = lambda kb, h: ct_ref[0, 0, kb, h:h + 1, :]

        def pair_body(kp, state):
            cks = [jnp.concatenate([ck(2 * kp, h), ck(2 * kp + 1, h)], axis=1) for h in range(2)]
            return block(pl.multiple_of(kp * 2 * TB, 2 * TB), 2 * TB, cks, None, state)

        def single_body(_, state):
            kb = qi - 1
            return block(pl.multiple_of(kb * TB, TB), TB, [ck(kb, h) for h in range(2)], None,
                         state)

        state = tuple((jnp.full((HQ, 1), NEG_BIG, F32), jnp.zeros((HQ, LANES), F32))
                      for _ in units)
        state = lax.fori_loop(0, lax.shift_right_logical(qi, 1), pair_body, state)
        state = lax.fori_loop(0, qi & 1, single_body, state)
        cks = [ck(qi, h)[:, 0:rows] for h in range(2)]
        masks = [_iota((HQ, rows), 0) + s * HQ >= _iota((HQ, rows), 1) for s in range(halves)]
        final = block(pl.multiple_of(qi * TB, TB), rows, cks, masks, state)
        for s in range(halves):
            acc0, acc1 = final[s][1], final[halves + s][1]
            out = jnp.where(lane_q < N, acc0 / acc0[:, N:N + 1], acc1 / acc1[:, 0:1])
            o_ref[0, s * HQ:(s + 1) * HQ, :] = out.astype(o_ref.dtype)

    if tail == 0:
        attend(TB)
    else:
        pl.when(qi < n_full)(lambda: attend(TB))
        pl.when(qi == n_full)(lambda: attend(tail))


def _fox(zq, ct, B, LP, width):
    n_pairs = width // LANES
    nq = pl.cdiv(LP, ATT_BLOCK)
    nkb = ct.shape[2]
    kv = lambda off: pl.BlockSpec((1, LP, LANES), lambda b, p, i: (b, 0, off + p))
    return pl.pallas_call(
        _fox_kernel,
        grid=(B, n_pairs, nq),
        in_specs=[pl.BlockSpec((1, ATT_BLOCK, LANES), lambda b, p, i: (b, i, p)),
                  kv(n_pairs), kv(2 * n_pairs),
                  pl.BlockSpec((1, 1, nkb, SUBLANES, ATT_BLOCK), lambda b, p, i: (b, p, 0, 0, 0))],
        out_specs=pl.BlockSpec((1, ATT_BLOCK, LANES), lambda b, p, i: (b, i, p)),
        out_shape=jax.ShapeDtypeStruct((B, LP, width), BF16),
        compiler_params=_cparams(("parallel", "parallel", "arbitrary")),
        name="fox_attention",
    )(zq, zq, zq, ct)


MLSTM_GATE_LANE_I = 12
MLSTM_GATE_LANE_F = 16


def _conv_silu(ref, r0, p0, notfirst, w_ref):
    cur = ref[0, pl.ds(r0, CHUNK), :].astype(F32)
    prev = ref[0, pl.ds(p0, BF16_ROWS), :][BF16_ROWS - SUBLANES:, :].astype(F32) * notfirst
    row8 = _iota(prev.shape, 0)
    y = cur * w_ref[MLSTM_CONV - 1:MLSTM_CONV, :]
    for s in range(1, MLSTM_CONV):
        rolled = pltpu.roll(cur, s, 0)
        top = jnp.where(row8 < s, pltpu.roll(prev, s, 0), rolled[0:SUBLANES])
        shifted = jnp.concatenate([top, rolled[SUBLANES:]], axis=0)
        y = y + shifted * w_ref[MLSTM_CONV - 1 - s:MLSTM_CONV - s, :]
    return y * _sigmoid(y)


def _mlstm_kernel(q_ref, k_ref, v_ref, og_ref, gr_ref, gc_ref, cwq_ref, cwk_ref,
                  o_ref, c_ref, m_ref):
    C, H = CHUNK, MLSTM_HEADS
    Dh = q_ref.shape[2] // H
    nchunk = q_ref.shape[1] // C
    c_ref[...] = jnp.zeros_like(c_ref)
    m_ref[...] = jnp.zeros_like(m_ref)

    incl = _iota((C, C), 0) >= _iota((C, C), 1)
    ones_col = (_iota((C, LANES), 1) == 0).astype(BF16)

    def body(c, carry):
        r0 = pl.multiple_of(c * C, C)
        p0 = pl.multiple_of(jnp.maximum(r0 - BF16_ROWS, 0), BF16_ROWS)
        notfirst = jnp.where(c > 0, 1.0, 0.0).astype(F32)
        q = _conv_silu(q_ref, r0, p0, notfirst, cwq_ref)
        k = _conv_silu(k_ref, r0, p0, notfirst, cwk_ref) * (Dh ** -0.5)
        v = v_ref[0, pl.ds(r0, C), :]
        og = og_ref[0, pl.ds(r0, C), :].astype(F32)
        g_rows = gr_ref[0, c]
        g_cols = gc_ref[0, pl.ds(r0, C), :]

        hs = range(H)
        sl = [slice(h * Dh, (h + 1) * Dh) for h in hs]
        m_all = m_ref[...]
        m_old = [m_all[h:h + 1, 0:1] for h in hs]
        c_old = [c_ref[h] for h in hs]
        li_r = [g_rows[h:h + 1, 0:C] for h in hs]
        b_r = [g_rows[H + h:H + h + 1, 0:C] for h in hs]
        li_c = [g_cols[:, MLSTM_GATE_LANE_I + h:MLSTM_GATE_LANE_I + h + 1] for h in hs]
        b_c = [g_cols[:, MLSTM_GATE_LANE_F + h:MLSTM_GATE_LANE_F + h + 1] for h in hs]
        qh = [q[:, sl[h]].astype(BF16) for h in hs]
        kh = [k[:, sl[h]] for h in hs]
        vaug = [jnp.concatenate([v[:, sl[h]], ones_col], axis=1) for h in hs]
        qk = [_dot_nt(qh[h], kh[h].astype(BF16)) for h in hs]
        qc = [_dot(qh[h], c_old[h].astype(BF16)) for h in hs]
        log_d = [jnp.where(incl, b_c[h] - b_r[h] + li_r[h], -jnp.inf) for h in hs]
        log_inter = [b_c[h] + m_old[h] for h in hs]
        m_q = [jnp.maximum(log_inter[h], jnp.max(log_d[h], axis=-1, keepdims=True)) for h in hs]
        s = [(qk[h] * jnp.exp(log_d[h] - m_q[h])).astype(BF16) for h in hs]
        b_end = [b_c[h][C - 1:C, :] for h in hs]
        m_new = [jnp.maximum(b_end[h] + m_old[h],
                             jnp.max(b_end[h] - b_r[h] + li_r[h], axis=-1, keepdims=True))
                 for h in hs]
        gk = [jnp.exp(b_end[h] - b_c[h] + li_c[h] - m_new[h]) for h in hs]
        num = [jnp.exp(log_inter[h] - m_q[h]) * qc[h] + _dot(s[h], vaug[h]) for h in hs]
        kv = [_dot_tn((gk[h] * kh[h]).astype(BF16), vaug[h]) for h in hs]
        outs = [_sigmoid(og[:, sl[h]]) * num[h][:, :Dh]
                / jnp.maximum(jnp.abs(num[h][:, Dh:Dh + 1]), jnp.exp(-m_q[h])) for h in hs]
        o_ref[0, pl.ds(r0, C), :] = jnp.concatenate(outs, axis=1).astype(o_ref.dtype)
        for h in hs:
            c_ref[h] = jnp.exp(b_end[h] + m_old[h] - m_new[h]) * c_old[h] + kv[h]
        m_ref[0:H, :] = jnp.concatenate([jnp.broadcast_to(m_new[h], (1, LANES)) for h in hs],
                                        axis=0)
        return carry

    lax.fori_loop(0, nchunk, body, 0)


def _mlstm(zm, g_rows, g_cols, cwq, cwk, B, LP):
    W = zm.shape[2] // 4
    Dh = W // MLSTM_HEADS
    seq = lambda j: pl.BlockSpec((1, LP, W), lambda b: (b, 0, j))
    cw = pl.BlockSpec((SUBLANES, W), lambda b: (0, 0))
    return pl.pallas_call(
        _mlstm_kernel,
        grid=(B,),
        in_specs=[seq(0), seq(1), seq(2), seq(3),
                  pl.BlockSpec((1, LP // CHUNK, SUBLANES, LANES), lambda b: (b, 0, 0, 0)),
                  pl.BlockSpec((1, LP, LANES), lambda b: (b, 0, 0)), cw, cw],
        out_specs=pl.BlockSpec((1, LP, W), lambda b: (b, 0, 0)),
        out_shape=jax.ShapeDtypeStruct((B, LP, W), BF16),
        scratch_shapes=[pltpu.VMEM((MLSTM_HEADS, Dh, 2 * Dh), F32),
                        pltpu.VMEM((SUBLANES, LANES), F32)],
        compiler_params=_cparams(("parallel",)),
        name="mlstm",
    )(zm, zm, zm, zm, g_rows, g_cols, cwq, cwk)


def _pad_rows(w, n):
    return jnp.pad(w, ((0, n - w.shape[0]), (0, 0)))


def kernel(x, meta_tokens, ln_emb_g, ln_emb_b, w_in, rwkv_mu, rwkv_w0, rwkv_w2, rwkv_a0, rwkv_a2, rwkv_g2, rwkv_k_k, rwkv_k_a, rwkv_r_k, rwkv_gn_g, rwkv_gn_b, fox_b_f, mlstm_conv_w, mlstm_b_i, mlstm_b_f, proj_rwkv, proj_fox, proj_mlstm, w_out, ln1_g, ln1_b, ffn_w1, ffn_w3, ffn_w2, router_w, moe_w1, moe_w3, moe_w2, ln2_g, ln2_b):
    B, S, D = x.shape
    L = N_META + S
    LP = -(-L // CHUNK) * CHUNK
    T = B * LP
    RW = proj_rwkv.shape[1]
    FW = proj_fox.shape[1]
    MW = proj_mlstm.shape[1]
    n_fox_heads = FW // FOX_HEAD_DIM
    dr, ar, gr = RWKV_DECAY_RANK, RWKV_AAA_RANK, RWKV_GATE_RANK
    c1 = 3 * RW + dr + ar + gr

    meta = jnp.broadcast_to(meta_tokens[None].astype(x.dtype), (B, N_META, D))
    hin = jnp.concatenate([meta, x, jnp.zeros((B, LP - L, D), x.dtype)], axis=1).reshape(T, D)
    h, hb = _ln(hin, ln_emb_g, ln_emb_b)

    ffn_b = tuple(w.astype(BF16) for w in (ffn_w1, ffn_w3, ffn_w2))
    moe_b = tuple(w.astype(BF16) for w in (moe_w1, moe_w3, moe_w2))

    for l in range(DEPTH):
        w_r, w_f, w_m, w_s, w_g = _prep_w_in(w_in, l, RW, FW, MW, dr, ar, gr)

        z_r = _mm(hb, w_r, BF16, w_r.shape[1], "in_rwkv").reshape(B, LP, -1)
        z_f = _mm(hb, w_f, BF16, w_f.shape[1], "in_fox").reshape(B, LP, -1)
        z_m = _mm(hb, w_m, BF16, w_m.shape[1], "in_mlstm").reshape(B, LP, -1)
        z_s = _mm(hb, w_s, F32, LANES, "in_gates").reshape(B, LP, LANES)

        mu = rwkv_mu[l]
        vecs = jnp.zeros((16, RW), F32)
        for i, p in enumerate([rwkv_w0[l], rwkv_a0[l], rwkv_k_k[l], rwkv_k_a[l],
                               rwkv_r_k[l].reshape(RW), rwkv_gn_g[l], rwkv_gn_b[l]]):
            vecs = vecs.at[i].set(p)
        vecs = vecs.at[8].set(mu[:RW]).at[9].set(mu[RW:2 * RW]).at[10].set(mu[2 * RW:3 * RW])
        mul = jnp.zeros((8, RWKV_LORA_W), F32)
        mul = mul.at[0, 0:dr].set(mu[3 * RW:3 * RW + dr])
        mul = mul.at[0, LANES:LANES + ar].set(mu[3 * RW + dr:3 * RW + dr + ar])
        mul = mul.at[0, 2 * LANES:2 * LANES + gr].set(mu[3 * RW + dr + ar:c1])
        y_r = _rwkv(z_r, vecs, mul, _pad_rows(rwkv_w2[l], LANES).astype(BF16),
                    _pad_rows(rwkv_a2[l], LANES).astype(BF16), rwkv_g2[l].astype(BF16), B, LP)

        bias = jnp.zeros((1, LANES), F32)
        bias = bias.at[0, 0:n_fox_heads].set(fox_b_f[l])
        bias = bias.at[0, MLSTM_GATE_LANE_I:MLSTM_GATE_LANE_I + MLSTM_HEADS].set(mlstm_b_i[l])
        bias = bias.at[0, MLSTM_GATE_LANE_F:MLSTM_GATE_LANE_F + MLSTM_HEADS].set(mlstm_b_f[l])

        ct, g_rows, g_cols = _gate_prep(z_s, bias, B, LP, FW // LANES)
        y_f = _fox(z_f, ct, B, LP, FW)

        cw = _pad_rows(mlstm_conv_w[l], SUBLANES)
        y_m = _mlstm(z_m, g_rows, g_cols, cw[:, :MW], cw[:, MW:], B, LP)

        pre = _merge(hb, w_g, y_r.reshape(T, RW), y_f.reshape(T, FW), y_m.reshape(T, MW),
                     proj_rwkv[l].astype(BF16), proj_fox[l].astype(BF16),
                     proj_mlstm[l].astype(BF16))
        h, hb = _mm_res_ln(pre, w_out[l].astype(BF16), h, ln1_g[l], ln1_b[l])

        i = l // 2
        if l % 2 == 0:
            h, hb = _ffn(hb, ffn_b, i, h, ln2_g[l], ln2_b[l])
        else:
            tm = _tile(T, MOE_TILE)
            n_tiles = 2 * T // tm + N_EXPERTS
            gw, ids = _router(h, router_w[i])
            row_token, dest, tile_expert, n_used = _route(ids, tm, n_tiles)
            xs = _gather_rows(h, row_token, n_used)
            ys = _moe_ffn(xs, moe_b, i, tile_expert, n_used, tm)
            h, hb = _combine(ys, dest, gw, h, ln2_g[l], ln2_b[l])

    return h.reshape(B, LP, D)[:, N_META:L]
```
